```python
import jax, jax.numpy as jnp
from jax import lax
import numpy as np

D_MODEL = 2048
BATCH = 8
SEQ = 4096
DEPTH = 1
DEC_BATCH = 32
DEC_SEQ = 32
PAST_LEN = 1024

CHUNK = 64
N_HEADS = 16
N_KV_HEADS = 4
HEAD_DIM = 64
GROUP = N_HEADS // N_KV_HEADS
ATTN_DIM = N_HEADS * HEAD_DIM
KV_DIM = N_KV_HEADS * HEAD_DIM
WINDOW = 128
WIN_CHUNKS = WINDOW // CHUNK
CONV_DIM = 1024
CONV_WIDTH = 31
ROPE_THETA = 10000.0
RMS_EPS = 1e-6
LN_EPS = 1e-5
NEG_INF = -1e30

Q_END = ATTN_DIM
K_END = Q_END + KV_DIM
V_END = K_END + KV_DIM
GA_END = V_END + ATTN_DIM
CU_END = GA_END + 2 * CONV_DIM
GB_END = CU_END + CONV_DIM
IN_DIM = GB_END + 2 * D_MODEL
IN_SPLITS = (Q_END, K_END, V_END, GA_END, CU_END, GB_END)

kernel_name = "hybrid_swa_sink_conformer_conv_stream_step"


def _rmsnorm(x, g):
    x32 = x.astype(jnp.float32)
    y = x32 * lax.rsqrt(jnp.mean(x32 * x32, axis=-1, keepdims=True) + RMS_EPS)
    return (y * g.astype(jnp.float32)).astype(x.dtype)


def _rope(x, pos):
    half = HEAD_DIM // 2
    inv = ROPE_THETA ** (-2.0 * jnp.arange(half, dtype=jnp.float32) / HEAD_DIM)
    ang = pos.astype(jnp.float32)[:, None] * inv[None, :]
    cos = jnp.cos(ang)[None, :, None, :]
    sin = jnp.sin(ang)[None, :, None, :]
    x1 = x[..., :half].astype(jnp.float32)
    x2 = x[..., half:].astype(jnp.float32)
    out = jnp.concatenate([x1 * cos - x2 * sin, x2 * cos + x1 * sin], axis=-1)
    return out.astype(x.dtype)


def _in_proj(x, c, norm_g, w_ada, b_ada, w_in, pos):
    mod = jax.nn.silu(c) @ w_ada + b_ada
    shift, scale, gate = jnp.split(mod[:, None, :], 3, axis=-1)
    h = _rmsnorm(x, norm_g) * (1 + scale) + shift
    z = h @ w_in
    q, k, v, ga, cu, gb, mg = jnp.split(z, IN_SPLITS, axis=-1)
    b, t, _ = x.shape
    q = _rope(q.reshape(b, t, N_HEADS, HEAD_DIM), pos)
    k = _rope(k.reshape(b, t, N_KV_HEADS, HEAD_DIM), pos)
    v = v.reshape(b, t, N_KV_HEADS, HEAD_DIM)
    u = cu[..., :CONV_DIM] * jax.nn.sigmoid(cu[..., CONV_DIM:])
    return q, k, v, ga, u, gb, mg, gate


def _attend(q, k, v, mask, sinks):
    s = jnp.einsum('...qkgd,...skd->...kgqs', q, k,
                   preferred_element_type=jnp.float32) * (HEAD_DIM ** -0.5)
    if mask is not None:
        s = jnp.where(mask, s, NEG_INF)
    sink = sinks.astype(jnp.float32).reshape(N_KV_HEADS, GROUP)[:, :, None, None]
    m = jnp.maximum(jnp.max(s, axis=-1, keepdims=True), sink)
    e = jnp.exp(s - m)
    p = e / (jnp.sum(e, axis=-1, keepdims=True) + jnp.exp(sink - m))
    return jnp.einsum('...kgqs,...skd->...qkgd', p.astype(v.dtype), v)


def _band_attention(q, k, v, sinks):
    b, s = q.shape[0], q.shape[1]
    nc = s // CHUNK
    qc = q.reshape(b, nc, CHUNK, N_KV_HEADS, GROUP, HEAD_DIM)
    kc = k.reshape(b, nc, CHUNK, N_KV_HEADS, HEAD_DIM)
    vc = v.reshape(b, nc, CHUNK, N_KV_HEADS, HEAD_DIM)
    padw = ((0, 0), (WIN_CHUNKS, 0), (0, 0), (0, 0), (0, 0))
    kp = jnp.pad(kc, padw)
    vp = jnp.pad(vc, padw)
    kb = jnp.concatenate([kp[:, i:i + nc] for i in range(WIN_CHUNKS + 1)], axis=2)
    vb = jnp.concatenate([vp[:, i:i + nc] for i in range(WIN_CHUNKS + 1)], axis=2)
    src = jnp.arange(nc)[:, None] - WIN_CHUNKS + jnp.arange(WIN_CHUNKS + 1)[None, :]
    valid = jnp.repeat(src >= 0, CHUNK, axis=1)
    mask = valid[:, None, None, None, :]
    o = _attend(qc, kb, vb, mask, sinks)
    return o.reshape(b, s, ATTN_DIM)


def _conv_branch(u_ctx, w_dw, b_dw, ln_g, ln_b):
    y = lax.conv_general_dilated(u_ctx, w_dw[:, None, :], window_strides=(1,),
                                 padding='VALID',
                                 dimension_numbers=('NWC', 'WIO', 'NWC'),
                                 feature_group_count=CONV_DIM) + b_dw
    y32 = y.astype(jnp.float32)
    mu = jnp.mean(y32, axis=-1, keepdims=True)
    var = jnp.mean(jnp.square(y32 - mu), axis=-1, keepdims=True)
    yn = (y32 - mu) * lax.rsqrt(var + LN_EPS) * ln_g.astype(jnp.float32) + ln_b.astype(jnp.float32)
    return jax.nn.silu(yn).astype(u_ctx.dtype)


def _out_proj(x, attn_o, conv_o, ga, gb, mg, gate, w_proj_a, w_proj_b, w_out):
    pa = (attn_o * jax.nn.silu(ga)) @ w_proj_a
    pb = (conv_o * jax.nn.silu(gb)) @ w_proj_b
    merged = jax.nn.sigmoid(mg[..., :D_MODEL]) * pa + jax.nn.sigmoid(mg[..., D_MODEL:]) * pb
    return x + gate * (merged @ w_out)


def setup_inputs(seed: int = 0) -> dict:
    key = jax.random.key(seed)
    ks = jax.random.split(key, 20)
    f32 = jnp.float32
    win_cache = min(WINDOW, PAST_LEN)
    nrm = lambda k, shp, s: jax.random.normal(k, shp, f32) * s
    return {
        "x_prompt": nrm(ks[0], (BATCH, SEQ, D_MODEL), 1.0),
        "x_sample": nrm(ks[1], (DEC_BATCH, DEC_SEQ, D_MODEL), 1.0),
        "c_prompt": nrm(ks[2], (BATCH, D_MODEL), 1.0),
        "c_sample": nrm(ks[3], (DEC_BATCH, D_MODEL), 1.0),
        "cache_k": nrm(ks[4], (DEPTH, DEC_BATCH, win_cache, N_KV_HEADS, HEAD_DIM), 1.0),
        "cache_v": nrm(ks[5], (DEPTH, DEC_BATCH, win_cache, N_KV_HEADS, HEAD_DIM), 1.0),
        "state_conv": nrm(ks[6], (DEPTH, DEC_BATCH, CONV_WIDTH - 1, CONV_DIM), 0.5),
        "norm_g": 1.0 + nrm(ks[7], (DEPTH, D_MODEL), 0.02),
        "w_ada": nrm(ks[8], (DEPTH, D_MODEL, 3 * D_MODEL), 0.5 * D_MODEL ** -0.5),
        "b_ada": nrm(ks[9], (DEPTH, 3 * D_MODEL), 0.02),
        "w_in": nrm(ks[10], (DEPTH, D_MODEL, IN_DIM), D_MODEL ** -0.5),
        "sinks": nrm(ks[11], (DEPTH, N_HEADS), 1.0),
        "w_dw": nrm(ks[12], (DEPTH, CONV_WIDTH, CONV_DIM), CONV_WIDTH ** -0.5),
        "b_dw": nrm(ks[13], (DEPTH, CONV_DIM), 0.02),
        "ln_g": 1.0 + nrm(ks[14], (DEPTH, CONV_DIM), 0.02),
        "ln_b": nrm(ks[15], (DEPTH, CONV_DIM), 0.02),
        "w_proj_a": nrm(ks[16], (DEPTH, ATTN_DIM, D_MODEL), ATTN_DIM ** -0.5),
        "w_proj_b": nrm(ks[17], (DEPTH, CONV_DIM, D_MODEL), CONV_DIM ** -0.5),
        "w_out": nrm(ks[18], (DEPTH, D_MODEL, D_MODEL), D_MODEL ** -0.5),
        "final_g": 1.0 + nrm(ks[19], (D_MODEL,), 0.02),
    }


def reference(x_prompt, x_sample, c_prompt, c_sample, cache_k, cache_v, state_conv,
              norm_g, w_ada, b_ada, w_in, sinks, w_dw, b_dw, ln_g, ln_b,
              w_proj_a, w_proj_b, w_out, final_g):
    seq = x_prompt.shape[1]
    dec_seq = x_sample.shape[1]
    n_win = cache_k.shape[2]
    pos_p = jnp.arange(seq)
    pos_s = PAST_LEN + jnp.arange(dec_seq)
    xp, xs = x_prompt, x_sample
    kp_l, vp_l, cp_l, ks_l, vs_l, cs_l = [], [], [], [], [], []
    for l in range(DEPTH):
        q, k, v, ga, u, gb, mg, gate = _in_proj(xp, c_prompt, norm_g[l], w_ada[l], b_ada[l], w_in[l], pos_p)
        attn_o = _band_attention(q, k, v, sinks[l])
        u_ctx = jnp.pad(u, ((0, 0), (CONV_WIDTH - 1, 0), (0, 0)))
        conv_o = _conv_branch(u_ctx, w_dw[l], b_dw[l], ln_g[l], ln_b[l])
        xp = _out_proj(xp, attn_o, conv_o, ga, gb, mg, gate, w_proj_a[l], w_proj_b[l], w_out[l])
        kp_l.append(k[:, -n_win:])
        vp_l.append(v[:, -n_win:])
        cp_l.append(u[:, -(CONV_WIDTH - 1):])
        q, k, v, ga, u, gb, mg, gate = _in_proj(xs, c_sample, norm_g[l], w_ada[l], b_ada[l], w_in[l], pos_s)
        k_all = jnp.concatenate([cache_k[l], k], axis=1)
        v_all = jnp.concatenate([cache_v[l], v], axis=1)
        qs = q.reshape(q.shape[0], dec_seq, N_KV_HEADS, GROUP, HEAD_DIM)
        attn_o = _attend(qs, k_all, v_all, None, sinks[l]).reshape(q.shape[0], dec_seq, ATTN_DIM)
        u_ctx = jnp.concatenate([state_conv[l], u], axis=1)
        conv_o = _conv_branch(u_ctx, w_dw[l], b_dw[l], ln_g[l], ln_b[l])
        xs = _out_proj(xs, attn_o, conv_o, ga, gb, mg, gate, w_proj_a[l], w_proj_b[l], w_out[l])
        ks_l.append(k_all[:, -n_win:])
        vs_l.append(v_all[:, -n_win:])
        cs_l.append(u_ctx[:, -(CONV_WIDTH - 1):])
    y_prompt = _rmsnorm(xp, final_g)
    y_sample = _rmsnorm(xs, final_g)
    new_k_prompt = jnp.stack(kp_l)
    new_v_prompt = jnp.stack(vp_l)
    new_conv_prompt = jnp.stack(cp_l)
    new_k_sample = jnp.stack(ks_l)
    new_v_sample = jnp.stack(vs_l)
    new_conv_sample = jnp.stack(cs_l)
    return (y_prompt, y_sample, new_k_prompt, new_v_prompt, new_conv_prompt, new_k_sample, new_v_sample, new_conv_sample)
```

```python
import functools

import jax
import jax.numpy as jnp
import numpy as np
from jax import lax
from jax.experimental import pallas as pl
from jax.experimental.pallas import tpu as pltpu

D_MODEL = 2048
N_HEADS = 16
N_KV = 4
HEAD_DIM = 64
HALF = HEAD_DIM // 2
GROUP = N_HEADS // N_KV
ATTN_DIM = N_HEADS * HEAD_DIM
KV_DIM = N_KV * HEAD_DIM
CONV_DIM = 1024
CONV_WIDTH = 31
CHUNK = 64
WINDOW = 128
ROPE_THETA = 10000.0
RMS_EPS = 1e-6
LN_EPS = 1e-5
NEG_INF = -1e30
PAST_LEN = 1024

Q_END = ATTN_DIM
K_END = Q_END + KV_DIM
V_END = K_END + KV_DIM
GA_END = V_END + ATTN_DIM
CU_END = GA_END + 2 * CONV_DIM
GB_END = CU_END + CONV_DIM
IN_DIM = GB_END + 2 * D_MODEL

LANES = 128
HIST_U = 32
N_CT = CONV_DIM // LANES
VMEM_LIMIT = 56 * 1024 * 1024

ZO_COLS = 8 * 1024
WBLK = 1024
N_WBLK = 9
ROW_SUB = 256
NORM_ROWS = 128

BF16 = jnp.bfloat16
F32 = jnp.float32


def _sigmoid(x):
    return 1.0 / (1.0 + jnp.exp(-x))


def _mod_kernel(c_ref, w_ref, b_ref, o_ref):
    c = c_ref[...]
    a = (c * _sigmoid(c)).astype(BF16)
    w = w_ref[...].astype(BF16)
    o_ref[...] = jnp.dot(a, w, preferred_element_type=F32) + b_ref[...]


def _mod_call(c_all, w_ada, b_ada):
    rows = c_all.shape[0]
    n = w_ada.shape[1]
    tn = 1024
    return pl.pallas_call(
        _mod_kernel,
        grid=(n // tn,),
        in_specs=[
            pl.BlockSpec((rows, D_MODEL), lambda j: (0, 0)),
            pl.BlockSpec((D_MODEL, tn), lambda j: (0, j)),
            pl.BlockSpec((1, tn), lambda j: (0, j)),
        ],
        out_specs=pl.BlockSpec((rows, tn), lambda j: (0, j)),
        out_shape=jax.ShapeDtypeStruct((rows, n), F32),
        compiler_params=pltpu.CompilerParams(
            dimension_semantics=("arbitrary",), vmem_limit_bytes=VMEM_LIMIT),
        name="adaln_mod",
    )(c_all, w_ada, b_ada)


def _rope_pair(a, b, cos, sin):
    return a * cos - b * sin, b * cos + a * sin


def _for_row_blocks(n_rows, blk, body):
    def step(n, carry):
        body(pl.multiple_of(n * blk, blk))
        return carry
    lax.fori_loop(0, n_rows // blk, step, 0)


def _inproj_kernel(x_ref, shift_ref, scale_ref, g_ref, cos_ref, sin_ref, w_ref, wkv_ref,
                   zo_ref, kv_ref, h_scr, glu_scr):
    j = pl.program_id(1)
    nb, t_rows, _ = x_ref.shape
    tm = nb * t_rows
    sub = min(tm, ROW_SUB)

    def proj(r0):
        return jnp.dot(h_scr[pl.ds(r0, sub), :], w_ref[...], preferred_element_type=F32)

    @pl.when(j == 0)
    def _():
        rc = min(t_rows, NORM_ROWS)
        per_b = t_rows // rc

        def norm_step(n, carry):
            bi = n // per_b
            r0 = pl.multiple_of((n % per_b) * rc, rc)
            x = x_ref[bi, pl.ds(r0, rc), :]
            ms = jnp.mean(x * x, axis=-1, keepdims=True)
            y = x * lax.rsqrt(ms + RMS_EPS) * g_ref[...]
            h = y * (1.0 + scale_ref[bi]) + shift_ref[bi]
            h_scr[pl.ds(pl.multiple_of(n * rc, rc), rc), :] = h.astype(BF16)
            return carry
        lax.fori_loop(0, nb * per_b, norm_step, 0)

        def qkv_rows(r0):
            rows = pl.ds(r0, sub)
            cos = cos_ref[rows, :]
            sin = sin_ref[rows, :]
            acc = jnp.dot(h_scr[rows, :], wkv_ref[...], preferred_element_type=F32)
            ka, kb = _rope_pair(acc[:, 0:LANES], acc[:, LANES:2 * LANES], cos, sin)
            kv_ref[rows, 0:LANES] = ka.astype(BF16)
            kv_ref[rows, LANES:2 * LANES] = kb.astype(BF16)
            kv_ref[rows, 2 * LANES:] = acc[:, 2 * LANES:].astype(BF16)
            acc = proj(r0)
            for i in range(GROUP):
                lo = i * 2 * LANES
                qa, qb = _rope_pair(acc[:, lo:lo + LANES], acc[:, lo + LANES:lo + 2 * LANES], cos, sin)
                zo_ref[rows, lo:lo + LANES] = qa.astype(BF16)
                zo_ref[rows, lo + LANES:lo + 2 * LANES] = qb.astype(BF16)
        _for_row_blocks(tm, sub, qkv_rows)

    @pl.when((j == 1) | (j == 4))
    def _():
        def silu_rows(r0):
            acc = proj(r0)
            zo_ref[pl.ds(r0, sub), :] = (acc * _sigmoid(acc)).astype(BF16)
        _for_row_blocks(tm, sub, silu_rows)

    @pl.when(j == 2)
    def _():
        def glu_a_rows(r0):
            glu_scr[pl.ds(r0, sub), :] = proj(r0)
        _for_row_blocks(tm, sub, glu_a_rows)

    @pl.when(j == 3)
    def _():
        def glu_b_rows(r0):
            zo_ref[pl.ds(r0, sub), :] = (glu_scr[pl.ds(r0, sub), :] * _sigmoid(proj(r0))).astype(BF16)
        _for_row_blocks(tm, sub, glu_b_rows)

    @pl.when(j >= 5)
    def _():
        def gate_rows(r0):
            zo_ref[pl.ds(r0, sub), :] = _sigmoid(proj(r0)).astype(BF16)
        _for_row_blocks(tm, sub, gate_rows)


def _inproj_call(x, shift, scale, norm_g, cos_t, sin_t, w_main, w_kv, *, nb, t_rows):
    b, s, _ = x.shape
    tm = nb * t_rows
    tiles_per_batch = s // t_rows
    n_tiles = (b // nb) * tiles_per_batch
    pos_tiles = cos_t.shape[0] // tm

    def x_map(i, j):
        return (i // tiles_per_batch, i % tiles_per_batch, 0)

    def mod_map(i, j):
        return (i // tiles_per_batch, 0, 0)

    def pos_map(i, j):
        return (i % pos_tiles, 0)

    def zo_map(i, j):
        return (i, jnp.where(j < 3, j, j - 1))

    return pl.pallas_call(
        _inproj_kernel,
        grid=(n_tiles, N_WBLK),
        in_specs=[
            pl.BlockSpec((nb, t_rows, D_MODEL), x_map),
            pl.BlockSpec((nb, 1, D_MODEL), mod_map),
            pl.BlockSpec((nb, 1, D_MODEL), mod_map),
            pl.BlockSpec((1, D_MODEL), lambda i, j: (0, 0)),
            pl.BlockSpec((tm, LANES), pos_map),
            pl.BlockSpec((tm, LANES), pos_map),
            pl.BlockSpec((D_MODEL, WBLK), lambda i, j: (0, j)),
            pl.BlockSpec((D_MODEL, 2 * KV_DIM), lambda i, j: (0, 0)),
        ],
        out_specs=[
            pl.BlockSpec((tm, WBLK), zo_map),
            pl.BlockSpec((tm, 2 * KV_DIM), lambda i, j: (i, 0)),
        ],
        out_shape=[
            jax.ShapeDtypeStruct((b * s, ZO_COLS), BF16),
            jax.ShapeDtypeStruct((b * s, 2 * KV_DIM), BF16),
        ],
        scratch_shapes=[
            pltpu.VMEM((tm, D_MODEL), BF16),
            pltpu.VMEM((tm, WBLK), F32),
        ],
        compiler_params=pltpu.CompilerParams(
            dimension_semantics=("arbitrary", "arbitrary"), vmem_limit_bytes=VMEM_LIMIT),
        name="in_proj",
    )(x, shift, scale, norm_g, cos_t, sin_t, w_main, w_kv)


def _kv_group_masks():
    lane = lax.broadcasted_iota(jnp.int32, (1, 2 * LANES), 1)
    kmask = [((lane % LANES) // HALF) == g for g in range(N_KV)]
    vmask = [(lane // HEAD_DIM) == g for g in range(N_KV)]
    return kmask, vmask


def _attend_chunk(qs, kwin, vwin, sinks_ref, invalid):
    rows = qs.shape[0]
    cq = rows // GROUP
    rid = lax.broadcasted_iota(jnp.int32, (rows, 1), 0)
    out = None
    for g in range(N_KV):
        s = lax.dot_general(qs, kwin[g], (((1,), (1,)), ((), ())), preferred_element_type=F32)
        s = s * (HEAD_DIM ** -0.5)
        if invalid is not None:
            s = jnp.where(invalid, NEG_INF, s)
        sink = jnp.full((rows, 1), sinks_ref[GROUP * g + GROUP - 1], F32)
        for i in range(GROUP - 2, -1, -1):
            sink = jnp.where(rid < (i + 1) * cq, sinks_ref[GROUP * g + i], sink)
        m = jnp.maximum(jnp.max(s, axis=-1, keepdims=True), sink)
        e = jnp.exp(s - m)
        denom = jnp.sum(e, axis=-1, keepdims=True) + jnp.exp(sink - m)
        p = (e / denom).astype(BF16)
        o = jnp.dot(p, vwin[g], preferred_element_type=F32)
        out = o if out is None else out + o
    return out


def _conv_ln_swish(ubuf, wdw_ref, bdw_ref, lng_ref, lnb_ref, y_scr, row0, n_rows):
    first = HIST_U - (CONV_WIDTH - 1)
    for ct in range(N_CT):
        cs = slice(ct * LANES, (ct + 1) * LANES)
        acc = jnp.broadcast_to(bdw_ref[:, cs], (n_rows, LANES))
        for k in range(CONV_WIDTH):
            acc = acc + wdw_ref[k:k + 1, cs] * ubuf[ct, pl.ds(first + k, n_rows), :]
        y_scr[pl.ds(row0, n_rows), cs] = acc
    y = y_scr[pl.ds(row0, n_rows), :]
    mu = jnp.mean(y, axis=-1, keepdims=True)
    yc = y - mu
    var = jnp.mean(yc * yc, axis=-1, keepdims=True)
    yn = yc * lax.rsqrt(var + LN_EPS) * lng_ref[...] + lnb_ref[...]
    return yn * _sigmoid(yn)


def _out_tail(x, gate, ao, co, smg, wa_ref, wb_ref, wo_ref, fg_ref):
    pa = jnp.dot(ao, wa_ref[...], preferred_element_type=F32)
    pb = jnp.dot(co, wb_ref[...], preferred_element_type=F32)
    merged = smg[:, :D_MODEL].astype(F32) * pa + smg[:, D_MODEL:].astype(F32) * pb
    o = jnp.dot(merged.astype(BF16), wo_ref[...], preferred_element_type=F32)
    r = x + gate * o
    ms = jnp.mean(r * r, axis=-1, keepdims=True)
    return r * lax.rsqrt(ms + RMS_EPS) * fg_ref[...]


def _mix_prompt_kernel(sinks_ref, x_ref, gate_ref, q_ref, sga_ref, u_ref, sgb_ref, smg_ref,
                       kv_ref, kvh_ref, uh_ref, wdw_ref, bdw_ref, lng_ref, lnb_ref,
                       wa_ref, wb_ref, wo_ref, fg_ref, y_ref,
                       km_scr, vm_scr, ubuf, cy_scr, ao_scr, co_scr):
    t = pl.program_id(1)
    tm = q_ref.shape[0]
    n_chunks = tm // CHUNK
    first_tile = t == 0

    kmask, vmask = _kv_group_masks()
    kh = kvh_ref[:, 0:KV_DIM]
    vh = kvh_ref[:, KV_DIM:]
    vh = jnp.where(first_tile, jnp.zeros_like(vh), vh)
    kc = kv_ref[:, 0:KV_DIM]
    vc = kv_ref[:, KV_DIM:]
    zero = jnp.zeros((), BF16)
    for g in range(N_KV):
        km_scr[g, 0:WINDOW, :] = jnp.where(kmask[g], kh, zero)
        km_scr[g, WINDOW:, :] = jnp.where(kmask[g], kc, zero)
        vm_scr[g, 0:WINDOW, :] = jnp.where(vmask[g], vh, zero)
        vm_scr[g, WINDOW:, :] = jnp.where(vmask[g], vc, zero)

    nk = WINDOW + CHUNK
    key_id = lax.broadcasted_iota(jnp.int32, (1, nk), 1)
    for c in range(n_chunks):
        r0 = c * CHUNK
        qs = jnp.concatenate(
            [q_ref[r0:r0 + CHUNK, i * 2 * LANES:(i + 1) * 2 * LANES] for i in range(GROUP)], axis=0)
        kwin = [km_scr[g, r0:r0 + nk, :] for g in range(N_KV)]
        vwin = [vm_scr[g, r0:r0 + nk, :] for g in range(N_KV)]
        n_bad = WINDOW - r0
        invalid = (first_tile & (key_id < n_bad)) if n_bad > 0 else None
        o = _attend_chunk(qs, kwin, vwin, sinks_ref, invalid)
        for i in range(GROUP):
            cs = slice(i * 2 * LANES, (i + 1) * 2 * LANES)
            ao_scr[r0:r0 + CHUNK, cs] = (
                o[i * CHUNK:(i + 1) * CHUNK, :] * sga_ref[r0:r0 + CHUNK, cs].astype(F32)).astype(BF16)

    uh = uh_ref[...].astype(F32)
    uh = jnp.where(first_tile, jnp.zeros_like(uh), uh)
    for ct in range(N_CT):
        cs = slice(ct * LANES, (ct + 1) * LANES)
        ubuf[ct, 0:HIST_U, :] = uh[:, cs]
        ubuf[ct, HIST_U:, :] = u_ref[:, cs].astype(F32)
    act = _conv_ln_swish(ubuf, wdw_ref, bdw_ref, lng_ref, lnb_ref, cy_scr, 0, tm)
    co_scr[...] = (act * sgb_ref[...].astype(F32)).astype(BF16)

    y_ref[0] = _out_tail(x_ref[0], gate_ref[0], ao_scr[...], co_scr[...], smg_ref[...],
                         wa_ref, wb_ref, wo_ref, fg_ref)


def _mix_sample_kernel(sinks_ref, x_ref, gate_ref, q_ref, sga_ref, u_ref, sgb_ref, smg_ref,
                       kv_ref, kvh_ref, uh_ref, wdw_ref, bdw_ref, lng_ref, lnb_ref,
                       wa_ref, wb_ref, wo_ref, fg_ref, y_ref,
                       ubuf, cy_scr, ao_scr, co_scr):
    nb, t_rows, _ = x_ref.shape
    kmask, vmask = _kv_group_masks()
    zero = jnp.zeros((), BF16)
    for b in range(nb):
        r0 = b * t_rows
        kall = jnp.concatenate([kvh_ref[b, :, 0:KV_DIM], kv_ref[r0:r0 + t_rows, 0:KV_DIM]], axis=0)
        vall = jnp.concatenate([kvh_ref[b, :, KV_DIM:], kv_ref[r0:r0 + t_rows, KV_DIM:]], axis=0)
        kwin = [jnp.where(kmask[g], kall, zero) for g in range(N_KV)]
        vwin = [jnp.where(vmask[g], vall, zero) for g in range(N_KV)]
        qs = jnp.concatenate(
            [q_ref[r0:r0 + t_rows, i * 2 * LANES:(i + 1) * 2 * LANES] for i in range(GROUP)], axis=0)
        o = _attend_chunk(qs, kwin, vwin, sinks_ref, None)
        for i in range(GROUP):
            cs = slice(i * 2 * LANES, (i + 1) * 2 * LANES)
            ao_scr[r0:r0 + t_rows, cs] = (
                o[i * t_rows:(i + 1) * t_rows, :] * sga_ref[r0:r0 + t_rows, cs].astype(F32)).astype(BF16)

        for ct in range(N_CT):
            cs = slice(ct * LANES, (ct + 1) * LANES)
            ubuf[ct, 0:HIST_U, :] = uh_ref[b, :, cs]
            ubuf[ct, HIST_U:, :] = u_ref[r0:r0 + t_rows, cs].astype(F32)
        act = _conv_ln_swish(ubuf, wdw_ref, bdw_ref, lng_ref, lnb_ref, cy_scr, r0, t_rows)
        co_scr[r0:r0 + t_rows, :] = (act * sgb_ref[r0:r0 + t_rows, :].astype(F32)).astype(BF16)

    x = x_ref[...]
    out = _out_tail(x.reshape(nb * t_rows, D_MODEL),
                    jnp.broadcast_to(gate_ref[...], x.shape).reshape(nb * t_rows, D_MODEL),
                    ao_scr[...], co_scr[...], smg_ref[...], wa_ref, wb_ref, wo_ref, fg_ref)
    y_ref[...] = out.reshape(x.shape)


def _const_spec(shape):
    nd = len(shape)
    return pl.BlockSpec(shape, lambda *_: (0,) * nd, pipeline_mode=pl.Buffered(1))


def _weight_specs():
    return [
        _const_spec((HIST_U, CONV_DIM)),
        _const_spec((1, CONV_DIM)),
        _const_spec((1, CONV_DIM)),
        _const_spec((1, CONV_DIM)),
        _const_spec((ATTN_DIM, D_MODEL)),
        _const_spec((CONV_DIM, D_MODEL)),
        _const_spec((D_MODEL, D_MODEL)),
        _const_spec((1, D_MODEL)),
    ]


def _mix_prompt_call(sinks, x, gate, zo, kv, weights, *, tm):
    b, s, _ = x.shape
    tpb = s // tm

    def tile(bi, t):
        return bi * tpb + t

    def kvh_map(bi, t):
        return (jnp.maximum(tile(bi, t) * (tm // WINDOW) - 1, 0), 0)

    def uh_map(bi, t):
        return (jnp.maximum(tile(bi, t) * (tm // HIST_U) - 1, 0), 2)

    in_specs = [
        pl.BlockSpec(memory_space=pltpu.SMEM),
        pl.BlockSpec((1, tm, D_MODEL), lambda bi, t: (bi, t, 0)),
        pl.BlockSpec((1, 1, D_MODEL), lambda bi, t: (bi, 0, 0)),
        pl.BlockSpec((tm, 1024), lambda bi, t: (tile(bi, t), 0)),
        pl.BlockSpec((tm, 1024), lambda bi, t: (tile(bi, t), 1)),
        pl.BlockSpec((tm, 1024), lambda bi, t: (tile(bi, t), 2)),
        pl.BlockSpec((tm, 1024), lambda bi, t: (tile(bi, t), 3)),
        pl.BlockSpec((tm, 4096), lambda bi, t: (tile(bi, t), 1)),
        pl.BlockSpec((tm, 2 * KV_DIM), lambda bi, t: (tile(bi, t), 0)),
        pl.BlockSpec((WINDOW, 2 * KV_DIM), kvh_map),
        pl.BlockSpec((HIST_U, 1024), uh_map),
    ] + _weight_specs()
    return pl.pallas_call(
        _mix_prompt_kernel,
        grid=(b, tpb),
        in_specs=in_specs,
        out_specs=pl.BlockSpec((1, tm, D_MODEL), lambda bi, t: (bi, t, 0)),
        out_shape=jax.ShapeDtypeStruct(x.shape, F32),
        scratch_shapes=[
            pltpu.VMEM((N_KV, WINDOW + tm, KV_DIM), BF16),
            pltpu.VMEM((N_KV, WINDOW + tm, KV_DIM), BF16),
            pltpu.VMEM((N_CT, HIST_U + tm, LANES), F32),
            pltpu.VMEM((tm, CONV_DIM), F32),
            pltpu.VMEM((tm, ATTN_DIM), BF16),
            pltpu.VMEM((tm, CONV_DIM), BF16),
        ],
        compiler_params=pltpu.CompilerParams(
            dimension_semantics=("arbitrary", "arbitrary"), vmem_limit_bytes=VMEM_LIMIT),
        name="mix_prompt",
    )(sinks, x, gate, zo, zo, zo, zo, zo, kv, kv, zo, *weights)


def _mix_sample_call(sinks, x, gate, zo, kv, kv_hist, u_hist, weights, *, nb):
    b, t_rows, _ = x.shape
    tm = nb * t_rows
    in_specs = [
        pl.BlockSpec(memory_space=pltpu.SMEM),
        pl.BlockSpec((nb, t_rows, D_MODEL), lambda i: (i, 0, 0)),
        pl.BlockSpec((nb, 1, D_MODEL), lambda i: (i, 0, 0)),
        pl.BlockSpec((tm, 1024), lambda i: (i, 0)),
        pl.BlockSpec((tm, 1024), lambda i: (i, 1)),
        pl.BlockSpec((tm, 1024), lambda i: (i, 2)),
        pl.BlockSpec((tm, 1024), lambda i: (i, 3)),
        pl.BlockSpec((tm, 4096), lambda i: (i, 1)),
        pl.BlockSpec((tm, 2 * KV_DIM), lambda i: (i, 0)),
        pl.BlockSpec((nb, WINDOW, 2 * KV_DIM), lambda i: (i, 0, 0)),
        pl.BlockSpec((nb, HIST_U, CONV_DIM), lambda i: (i, 0, 0)),
    ] + _weight_specs()
    return pl.pallas_call(
        _mix_sample_kernel,
        grid=(b // nb,),
        in_specs=in_specs,
        out_specs=pl.BlockSpec((nb, t_rows, D_MODEL), lambda i: (i, 0, 0)),
        out_shape=jax.ShapeDtypeStruct(x.shape, F32),
        scratch_shapes=[
            pltpu.VMEM((N_CT, HIST_U + t_rows, LANES), F32),
            pltpu.VMEM((tm, CONV_DIM), F32),
            pltpu.VMEM((tm, ATTN_DIM), BF16),
            pltpu.VMEM((tm, CONV_DIM), BF16),
        ],
        compiler_params=pltpu.CompilerParams(
            dimension_semantics=("arbitrary",), vmem_limit_bytes=VMEM_LIMIT),
        name="mix_sample",
    )(sinks, x, gate, zo, zo, zo, zo, zo, kv, kv_hist, u_hist, *weights)


def _q_cols(w):
    lead = w.shape[:-1]
    w = w.reshape(*lead, N_KV, GROUP, 2, HALF)
    return jnp.moveaxis(w, (-4, -3, -2), (-2, -4, -3)).reshape(*lead, ATTN_DIM)


def _k_cols(w):
    lead = w.shape[:-1]
    w = w.reshape(*lead, N_KV, 2, HALF)
    return jnp.swapaxes(w, -3, -2).reshape(*lead, KV_DIM)


def _k_cols_inv(w):
    lead = w.shape[:-1]
    w = w.reshape(*lead, 2, N_KV, HALF)
    return jnp.swapaxes(w, -3, -2).reshape(*lead, KV_DIM)


def _a_cols(w):
    lead = w.shape[:-1]
    w = w.reshape(*lead, N_KV, GROUP, HEAD_DIM)
    return jnp.swapaxes(w, -3, -2).reshape(*lead, ATTN_DIM)


def _rope_tables(pos):
    inv = ROPE_THETA ** (-2.0 * jnp.arange(HALF, dtype=F32) / HEAD_DIM)
    ang = pos.astype(F32)[:, None] * inv[None, :]
    return jnp.tile(jnp.cos(ang), (1, N_KV)), jnp.tile(jnp.sin(ang), (1, N_KV))


def _layer(l, xp, xs, c_prompt, c_sample, cache_k, cache_v, state_conv, norm_g, w_ada, b_ada, w_in,
           sinks, w_dw, b_dw, ln_g, ln_b, w_proj_a, w_proj_b, w_out, final_g):
    b, s, _ = xp.shape
    db, ds, _ = xs.shape
    n_win = cache_k.shape[2]
    in_tm = min(s, 1024)
    assert n_win == WINDOW and s % in_tm == 0 and s % 256 == 0 and db % 8 == 0 and ds % 16 == 0

    wi = w_in[l]
    w_main = jnp.concatenate(
        [_q_cols(wi[:, :Q_END]), _a_cols(wi[:, V_END:GA_END]), wi[:, GA_END:]], axis=1).astype(BF16)
    w_kv = jnp.concatenate([_k_cols(wi[:, Q_END:K_END]), wi[:, K_END:V_END]], axis=1).astype(BF16)
    wa = jnp.swapaxes(w_proj_a[l].reshape(N_KV, GROUP, HEAD_DIM, D_MODEL), 0, 1)
    wa = wa.reshape(ATTN_DIM, D_MODEL).astype(BF16)
    wb = w_proj_b[l].astype(BF16)
    wo = w_out[l].astype(BF16)
    wdw = jnp.pad(w_dw[l], ((0, HIST_U - CONV_WIDTH), (0, 0)))
    weights = (wdw, b_dw[l][None], ln_g[l][None], ln_b[l][None], wa, wb, wo, final_g[None, :])

    pad_rows = (-(b + db)) % 16
    c_all = jnp.concatenate([c_prompt, c_sample, jnp.zeros((pad_rows, D_MODEL), F32)], axis=0)
    mod = _mod_call(c_all, w_ada[l], b_ada[l][None])
    shift, scale, gate = (mod[:, i * D_MODEL:(i + 1) * D_MODEL][:, None, :] for i in range(3))

    cos_p, sin_p = _rope_tables(jnp.arange(s))
    cos_s, sin_s = _rope_tables(PAST_LEN + jnp.arange(ds))
    cos_s, sin_s = jnp.tile(cos_s, (db, 1)), jnp.tile(sin_s, (db, 1))

    zo_p, kv_p = _inproj_call(xp, shift[:b], scale[:b], norm_g[l][None], cos_p, sin_p, w_main, w_kv,
                              nb=1, t_rows=in_tm)
    yp = _mix_prompt_call(sinks[l], xp, gate[:b], zo_p, kv_p, weights, tm=256)
    kv3 = kv_p.reshape(b, s, 2 * KV_DIM)[:, s - n_win:].astype(F32)
    new_k_p = _k_cols_inv(kv3[..., :KV_DIM]).reshape(b, n_win, N_KV, HEAD_DIM)
    new_v_p = kv3[..., KV_DIM:].reshape(b, n_win, N_KV, HEAD_DIM)
    new_c_p = zo_p.reshape(b, s, ZO_COLS)[:, s - (CONV_WIDTH - 1):, 2048:3072].astype(F32)

    zo_s, kv_s = _inproj_call(xs, shift[b:b + db], scale[b:b + db], norm_g[l][None], cos_s, sin_s,
                              w_main, w_kv, nb=db, t_rows=ds)
    kv_hist = jnp.concatenate(
        [_k_cols(cache_k[l].reshape(db, n_win, KV_DIM)), cache_v[l].reshape(db, n_win, KV_DIM)],
        axis=-1).astype(BF16)
    u_hist = jnp.pad(state_conv[l], ((0, 0), (HIST_U - (CONV_WIDTH - 1), 0), (0, 0)))
    ys = _mix_sample_call(sinks[l], xs, gate[b:b + db], zo_s, kv_s, kv_hist, u_hist, weights, nb=8)
    kvs3 = kv_s.reshape(db, ds, 2 * KV_DIM).astype(F32)
    k_new = _k_cols_inv(kvs3[..., :KV_DIM]).reshape(db, ds, N_KV, HEAD_DIM)
    v_new = kvs3[..., KV_DIM:].reshape(db, ds, N_KV, HEAD_DIM)
    new_k_s = jnp.concatenate([cache_k[l], k_new], axis=1)[:, -n_win:]
    new_v_s = jnp.concatenate([cache_v[l], v_new], axis=1)[:, -n_win:]
    u_new = zo_s.reshape(db, ds, ZO_COLS)[:, :, 2048:3072].astype(F32)
    new_c_s = jnp.concatenate([state_conv[l], u_new], axis=1)[:, -(CONV_WIDTH - 1):]
    return yp, ys, new_k_p, new_v_p, new_c_p, new_k_s, new_v_s, new_c_s


def kernel(x_prompt, x_sample, c_prompt, c_sample, cache_k, cache_v, state_conv, norm_g, w_ada, b_ada,
           w_in, sinks, w_dw, b_dw, ln_g, ln_b, w_proj_a, w_proj_b, w_out, final_g):
    assert w_in.shape[0] == 1
    res = _layer(0, x_prompt, x_sample, c_prompt, c_sample, cache_k, cache_v, state_conv, norm_g, w_ada,
                 b_ada, w_in, sinks, w_dw, b_dw, ln_g, ln_b, w_proj_a, w_proj_b, w_out, final_g)
    return res[:2] + tuple(r[None] for r in res[2:])
```

```python
import jax
import jax.numpy as jnp
from jax import lax
from jax.experimental import pallas as pl
from jax.experimental.pallas import tpu as pltpu

D_MODEL = 2048
N_HEADS = 16
N_KV = 4
HEAD_DIM = 64
HALF = HEAD_DIM // 2
GROUP = N_HEADS // N_KV
ATTN_DIM = N_HEADS * HEAD_DIM
KV_DIM = N_KV * HEAD_DIM
CONV_DIM = 1024
CONV_WIDTH = 31
CHUNK = 64
WINDOW = 128
ROPE_THETA = 10000.0
RMS_EPS = 1e-6
LN_EPS = 1e-5
NEG_INF = -1e30
PAST_LEN = 1024

Q_END = ATTN_DIM
K_END = Q_END + KV_DIM
V_END = K_END + KV_DIM
GA_END = V_END + ATTN_DIM
CU_END = GA_END + 2 * CONV_DIM
GB_END = CU_END + CONV_DIM
IN_DIM = GB_END + 2 * D_MODEL

LANES = 128
HIST_U = 32
N_CT = CONV_DIM // LANES
VMEM_LIMIT = 56 * 1024 * 1024

ZO_COLS = 8 * 1024
WBLK = 1024
N_HEAD_BLK = 2
N_WBLK = 9
ROW_SUB = 256
NORM_ROWS = 128

BF16 = jnp.bfloat16
F32 = jnp.float32


def _sigmoid(x):
    return 1.0 / (1.0 + jnp.exp(-x))


def _mod_kernel(c_ref, w_ref, b_ref, o_ref):
    c = c_ref[...]
    a = (c * _sigmoid(c)).astype(BF16)
    w = w_ref[...].astype(BF16)
    o_ref[...] = jnp.dot(a, w, preferred_element_type=F32) + b_ref[...]


def _mod_call(c_all, w_ada, b_ada):
    rows = c_all.shape[0]
    n = w_ada.shape[1]
    tn = 1024
    return pl.pallas_call(
        _mod_kernel,
        grid=(n // tn,),
        in_specs=[
            pl.BlockSpec((rows, D_MODEL), lambda j: (0, 0)),
            pl.BlockSpec((D_MODEL, tn), lambda j: (0, j)),
            pl.BlockSpec((1, tn), lambda j: (0, j)),
        ],
        out_specs=pl.BlockSpec((rows, tn), lambda j: (0, j)),
        out_shape=jax.ShapeDtypeStruct((rows, n), F32),
        compiler_params=pltpu.CompilerParams(
            dimension_semantics=("arbitrary",), vmem_limit_bytes=VMEM_LIMIT),
        name="adaln_mod",
    )(c_all, w_ada, b_ada)


def _rope_pair(a, b, cos, sin):
    return a * cos - b * sin, b * cos + a * sin


def _inproj_kernel(x_ref, shift_ref, scale_ref, g_ref, cos_ref, sin_ref, wh_ref, wr_ref, wkv_ref,
                   zo_ref, kv_ref, h_scr, glu_scr):
    j = pl.program_id(1)
    nb, t_rows, _ = x_ref.shape
    tm = nb * t_rows
    sub = min(tm, ROW_SUB)
    row_blocks = range(0, tm, sub)

    def proj(w_ref, r0):
        return jnp.dot(h_scr[r0:r0 + sub, :], w_ref[...], preferred_element_type=F32)

    def norm_rows(r0, rc):
        if rc <= t_rows:
            bi, t0 = divmod(r0, t_rows)
            x = x_ref[bi, t0:t0 + rc, :]
            scale, shift = scale_ref[bi], shift_ref[bi]
        else:
            b0, nbc = r0 // t_rows, rc // t_rows
            x = x_ref[b0:b0 + nbc]
            scale, shift = scale_ref[b0:b0 + nbc], shift_ref[b0:b0 + nbc]
        ms = jnp.mean(x * x, axis=-1, keepdims=True)
        y = x * lax.rsqrt(ms + RMS_EPS) * g_ref[...]
        h = y * (1.0 + scale) + shift
        h_scr[r0:r0 + rc, :] = h.reshape(rc, D_MODEL).astype(BF16)

    @pl.when(j == 0)
    def _():
        rc = min(sub, NORM_ROWS)
        for r0 in row_blocks:
            for c0 in range(r0, r0 + sub, rc):
                norm_rows(c0, rc)
            rows = slice(r0, r0 + sub)
            cos = cos_ref[rows, :]
            sin = sin_ref[rows, :]
            acc = jnp.dot(h_scr[rows, :], wkv_ref[...], preferred_element_type=F32)
            ka, kb = _rope_pair(acc[:, 0:LANES], acc[:, LANES:2 * LANES], cos, sin)
            kv_ref[rows, 0:LANES] = ka.astype(BF16)
            kv_ref[rows, LANES:2 * LANES] = kb.astype(BF16)
            kv_ref[rows, 2 * LANES:] = acc[:, 2 * LANES:].astype(BF16)
            acc = proj(wh_ref, r0)
            for i in range(GROUP):
                lo = i * 2 * LANES
                qa, qb = _rope_pair(acc[:, lo:lo + LANES], acc[:, lo + LANES:lo + 2 * LANES], cos, sin)
                zo_ref[rows, lo:lo + LANES] = qa.astype(BF16)
                zo_ref[rows, lo + LANES:lo + 2 * LANES] = qb.astype(BF16)

    @pl.when(j == 1)
    def _():
        for r0 in row_blocks:
            acc = proj(wh_ref, r0)
            zo_ref[r0:r0 + sub, :] = (acc * _sigmoid(acc)).astype(BF16)

    @pl.when(j == 2)
    def _():
        for r0 in row_blocks:
            glu_scr[r0:r0 + sub, :] = proj(wr_ref, r0)

    @pl.when(j == 3)
    def _():
        for r0 in row_blocks:
            zo_ref[r0:r0 + sub, :] = (glu_scr[r0:r0 + sub, :] * _sigmoid(proj(wr_ref, r0))).astype(BF16)

    @pl.when(j == 4)
    def _():
        for r0 in row_blocks:
            acc = proj(wr_ref, r0)
            zo_ref[r0:r0 + sub, :] = (acc * _sigmoid(acc)).astype(BF16)

    @pl.when(j >= 5)
    def _():
        for r0 in row_blocks:
            zo_ref[r0:r0 + sub, :] = _sigmoid(proj(wr_ref, r0)).astype(BF16)


def _inproj_call(x, shift, scale, norm_g, cos_t, sin_t, w_head, w_rest, w_kv, *, nb, t_rows):
    b, s, _ = x.shape
    tm = nb * t_rows
    tiles_per_batch = s // t_rows
    n_tiles = (b // nb) * tiles_per_batch
    pos_tiles = cos_t.shape[0] // tm

    def x_map(i, j):
        return (i // tiles_per_batch, i % tiles_per_batch, 0)

    def mod_map(i, j):
        return (i // tiles_per_batch, 0, 0)

    def pos_map(i, j):
        return (i % pos_tiles, 0)

    def zo_map(i, j):
        return (i, jnp.where(j < 3, j, j - 1))

    return pl.pallas_call(
        _inproj_kernel,
        grid=(n_tiles, N_WBLK),
        in_specs=[
            pl.BlockSpec((nb, t_rows, D_MODEL), x_map),
            pl.BlockSpec((nb, 1, D_MODEL), mod_map),
            pl.BlockSpec((nb, 1, D_MODEL), mod_map),
            pl.BlockSpec((1, D_MODEL), lambda i, j: (0, 0)),
            pl.BlockSpec((tm, LANES), pos_map),
            pl.BlockSpec((tm, LANES), pos_map),
            pl.BlockSpec((D_MODEL, WBLK), lambda i, j: (0, jnp.minimum(j, N_HEAD_BLK - 1))),
            pl.BlockSpec((D_MODEL, WBLK), lambda i, j: (0, jnp.maximum(j - N_HEAD_BLK, 0))),
            pl.BlockSpec((D_MODEL, 2 * KV_DIM), lambda i, j: (0, 0)),
        ],
        out_specs=[
            pl.BlockSpec((tm, WBLK), zo_map),
            pl.BlockSpec((tm, 2 * KV_DIM), lambda i, j: (i, 0)),
        ],
        out_shape=[
            jax.ShapeDtypeStruct((b * s, ZO_COLS), BF16),
            jax.ShapeDtypeStruct((b * s, 2 * KV_DIM), BF16),
        ],
        scratch_shapes=[
            pltpu.VMEM((tm, D_MODEL), BF16),
            pltpu.VMEM((tm, WBLK), F32),
        ],
        compiler_params=pltpu.CompilerParams(
            dimension_semantics=("arbitrary", "arbitrary"), vmem_limit_bytes=VMEM_LIMIT),
        name="in_proj",
    )(x, shift, scale, norm_g, cos_t, sin_t, w_head, w_rest, w_kv)


def _kv_group_masks():
    lane = lax.broadcasted_iota(jnp.int32, (1, 2 * LANES), 1)
    kmask = [((lane % LANES) // HALF) == g for g in range(N_KV)]
    vmask = [(lane // HEAD_DIM) == g for g in range(N_KV)]
    return kmask, vmask


def _attend_chunk(qs, kwin, vwin, sinks_ref, invalid):
    rows = qs.shape[0]
    cq = rows // GROUP
    rid = lax.broadcasted_iota(jnp.int32, (rows, 1), 0)
    out = None
    for g in range(N_KV):
        s = lax.dot_general(qs, kwin[g], (((1,), (1,)), ((), ())), preferred_element_type=F32)
        s = s * (HEAD_DIM ** -0.5)
        if invalid is not None:
            s = jnp.where(invalid, NEG_INF, s)
        sink = jnp.full((rows, 1), sinks_ref[GROUP * g + GROUP - 1], F32)
        for i in range(GROUP - 2, -1, -1):
            sink = jnp.where(rid < (i + 1) * cq, sinks_ref[GROUP * g + i], sink)
        m = jnp.maximum(jnp.max(s, axis=-1, keepdims=True), sink)
        e = jnp.exp(s - m)
        denom = jnp.sum(e, axis=-1, keepdims=True) + jnp.exp(sink - m)
        p = (e / denom).astype(BF16)
        o = jnp.dot(p, vwin[g], preferred_element_type=F32)
        out = o if out is None else out + o
    return out


def _conv_ln_swish(ubuf, wdw_ref, bdw_ref, lng_ref, lnb_ref, y_scr, row0, n_rows):
    first = HIST_U - (CONV_WIDTH - 1)
    for ct in range(N_CT):
        cs = slice(ct * LANES, (ct + 1) * LANES)
        acc = jnp.broadcast_to(bdw_ref[:, cs], (n_rows, LANES))
        for k in range(CONV_WIDTH):
            acc = acc + wdw_ref[k:k + 1, cs] * ubuf[ct, pl.ds(first + k, n_rows), :]
        y_scr[pl.ds(row0, n_rows), cs] = acc
    y = y_scr[pl.ds(row0, n_rows), :]
    mu = jnp.mean(y, axis=-1, keepdims=True)
    yc = y - mu
    var = jnp.mean(yc * yc, axis=-1, keepdims=True)
    yn = yc * lax.rsqrt(var + LN_EPS) * lng_ref[...] + lnb_ref[...]
    return yn * _sigmoid(yn)


def _out_tail(x, gate, ao, co, smg, wa_ref, wb_ref, wo_ref, fg_ref):
    pa = jnp.dot(ao, wa_ref[...], preferred_element_type=F32)
    pb = jnp.dot(co, wb_ref[...], preferred_element_type=F32)
    merged = smg[:, :D_MODEL].astype(F32) * pa + smg[:, D_MODEL:].astype(F32) * pb
    o = jnp.dot(merged.astype(BF16), wo_ref[...], preferred_element_type=F32)
    r = x + gate * o
    ms = jnp.mean(r * r, axis=-1, keepdims=True)
    return r * lax.rsqrt(ms + RMS_EPS) * fg_ref[...]


def _mix_prompt_kernel(sinks_ref, x_ref, gate_ref, q_ref, sga_ref, u_ref, sgb_ref, smg_ref,
                       kv_ref, kvh_ref, uh_ref, wdw_ref, bdw_ref, lng_ref, lnb_ref,
                       wa_ref, wb_ref, wo_ref, fg_ref, y_ref,
                       km_scr, vm_scr, ubuf, cy_scr, ao_scr, co_scr):
    t = pl.program_id(1)
    tm = q_ref.shape[0]
    n_chunks = tm // CHUNK
    first_tile = t == 0

    kmask, vmask = _kv_group_masks()
    kh = kvh_ref[:, 0:KV_DIM]
    vh = kvh_ref[:, KV_DIM:]
    vh = jnp.where(first_tile, jnp.zeros_like(vh), vh)
    kc = kv_ref[:, 0:KV_DIM]
    vc = kv_ref[:, KV_DIM:]
    zero = jnp.zeros((), BF16)
    for g in range(N_KV):
        km_scr[g, 0:WINDOW, :] = jnp.where(kmask[g], kh, zero)
        km_scr[g, WINDOW:, :] = jnp.where(kmask[g], kc, zero)
        vm_scr[g, 0:WINDOW, :] = jnp.where(vmask[g], vh, zero)
        vm_scr[g, WINDOW:, :] = jnp.where(vmask[g], vc, zero)

    nk = WINDOW + CHUNK
    key_id = lax.broadcasted_iota(jnp.int32, (1, nk), 1)
    for c in range(n_chunks):
        r0 = c * CHUNK
        qs = jnp.concatenate(
            [q_ref[r0:r0 + CHUNK, i * 2 * LANES:(i + 1) * 2 * LANES] for i in range(GROUP)], axis=0)
        kwin = [km_scr[g, r0:r0 + nk, :] for g in range(N_KV)]
        vwin = [vm_scr[g, r0:r0 + nk, :] for g in range(N_KV)]
        n_bad = WINDOW - r0
        invalid = (first_tile & (key_id < n_bad)) if n_bad > 0 else None
        o = _attend_chunk(qs, kwin, vwin, sinks_ref, invalid)
        for i in range(GROUP):
            cs = slice(i * 2 * LANES, (i + 1) * 2 * LANES)
            ao_scr[r0:r0 + CHUNK, cs] = (
                o[i * CHUNK:(i + 1) * CHUNK, :] * sga_ref[r0:r0 + CHUNK, cs].astype(F32)).astype(BF16)

    uh = uh_ref[...].astype(F32)
    uh = jnp.where(first_tile, jnp.zeros_like(uh), uh)
    for ct in range(N_CT):
        cs = slice(ct * LANES, (ct + 1) * LANES)
        ubuf[ct, 0:HIST_U, :] = uh[:, cs]
        ubuf[ct, HIST_U:, :] = u_ref[:, cs].astype(F32)
    act = _conv_ln_swish(ubuf, wdw_ref, bdw_ref, lng_ref, lnb_ref, cy_scr, 0, tm)
    co_scr[...] = (act * sgb_ref[...].astype(F32)).astype(BF16)

    y_ref[0] = _out_tail(x_ref[0], gate_ref[0], ao_scr[...], co_scr[...], smg_ref[...],
                         wa_ref, wb_ref, wo_ref, fg_ref)


def _mix_sample_kernel(sinks_ref, x_ref, gate_ref, q_ref, sga_ref, u_ref, sgb_ref, smg_ref,
                       kv_ref, kvh_ref, uh_ref, wdw_ref, bdw_ref, lng_ref, lnb_ref,
                       wa_ref, wb_ref, wo_ref, fg_ref, y_ref,
                       ubuf, cy_scr, ao_scr, co_scr):
    nb, t_rows, _ = x_ref.shape
    kmask, vmask = _kv_group_masks()
    zero = jnp.zeros((), BF16)
    for b in range(nb):
        r0 = b * t_rows
        kall = jnp.concatenate([kvh_ref[b, :, 0:KV_DIM], kv_ref[r0:r0 + t_rows, 0:KV_DIM]], axis=0)
        vall = jnp.concatenate([kvh_ref[b, :, KV_DIM:], kv_ref[r0:r0 + t_rows, KV_DIM:]], axis=0)
        kwin = [jnp.where(kmask[g], kall, zero) for g in range(N_KV)]
        vwin = [jnp.where(vmask[g], vall, zero) for g in range(N_KV)]
        qs = jnp.concatenate(
            [q_ref[r0:r0 + t_rows, i * 2 * LANES:(i + 1) * 2 * LANES] for i in range(GROUP)], axis=0)
        o = _attend_chunk(qs, kwin, vwin, sinks_ref, None)
        for i in range(GROUP):
            cs = slice(i * 2 * LANES, (i + 1) * 2 * LANES)
            ao_scr[r0:r0 + t_rows, cs] = (
                o[i * t_rows:(i + 1) * t_rows, :] * sga_ref[r0:r0 + t_rows, cs].astype(F32)).astype(BF16)

        for ct in range(N_CT):
            cs = slice(ct * LANES, (ct + 1) * LANES)
            ubuf[ct, 0:HIST_U, :] = uh_ref[b, :, cs]
            ubuf[ct, HIST_U:, :] = u_ref[r0:r0 + t_rows, cs].astype(F32)
        act = _conv_ln_swish(ubuf, wdw_ref, bdw_ref, lng_ref, lnb_ref, cy_scr, r0, t_rows)
        co_scr[r0:r0 + t_rows, :] = (act * sgb_ref[r0:r0 + t_rows, :].astype(F32)).astype(BF16)

    x = x_ref[...]
    out = _out_tail(x.reshape(nb * t_rows, D_MODEL),
                    jnp.broadcast_to(gate_ref[...], x.shape).reshape(nb * t_rows, D_MODEL),
                    ao_scr[...], co_scr[...], smg_ref[...], wa_ref, wb_ref, wo_ref, fg_ref)
    y_ref[...] = out.reshape(x.shape)


def _const_spec(shape):
    nd = len(shape)
    return pl.BlockSpec(shape, lambda *_: (0,) * nd, pipeline_mode=pl.Buffered(1))


def _weight_specs():
    return [
        _const_spec((HIST_U, CONV_DIM)),
        _const_spec((1, CONV_DIM)),
        _const_spec((1, CONV_DIM)),
        _const_spec((1, CONV_DIM)),
        _const_spec((ATTN_DIM, D_MODEL)),
        _const_spec((CONV_DIM, D_MODEL)),
        _const_spec((D_MODEL, D_MODEL)),
        _const_spec((1, D_MODEL)),
    ]


def _mix_prompt_call(sinks, x, gate, zo, kv, weights, *, tm):
    b, s, _ = x.shape
    tpb = s // tm

    def tile(bi, t):
        return bi * tpb + t

    def kvh_map(bi, t):
        return (jnp.maximum(tile(bi, t) * (tm // WINDOW) - 1, 0), 0)

    def uh_map(bi, t):
        return (jnp.maximum(tile(bi, t) * (tm // HIST_U) - 1, 0), 2)

    in_specs = [
        pl.BlockSpec(memory_space=pltpu.SMEM),
        pl.BlockSpec((1, tm, D_MODEL), lambda bi, t: (bi, t, 0)),
        pl.BlockSpec((1, 1, D_MODEL), lambda bi, t: (bi, 0, 0)),
        pl.BlockSpec((tm, 1024), lambda bi, t: (tile(bi, t), 0)),
        pl.BlockSpec((tm, 1024), lambda bi, t: (tile(bi, t), 1)),
        pl.BlockSpec((tm, 1024), lambda bi, t: (tile(bi, t), 2)),
        pl.BlockSpec((tm, 1024), lambda bi, t: (tile(bi, t), 3)),
        pl.BlockSpec((tm, 4096), lambda bi, t: (tile(bi, t), 1)),
        pl.BlockSpec((tm, 2 * KV_DIM), lambda bi, t: (tile(bi, t), 0)),
        pl.BlockSpec((WINDOW, 2 * KV_DIM), kvh_map),
        pl.BlockSpec((HIST_U, 1024), uh_map),
    ] + _weight_specs()
    return pl.pallas_call(
        _mix_prompt_kernel,
        grid=(b, tpb),
        in_specs=in_specs,
        out_specs=pl.BlockSpec((1, tm, D_MODEL), lambda bi, t: (bi, t, 0)),
        out_shape=jax.ShapeDtypeStruct(x.shape, F32),
        scratch_shapes=[
            pltpu.VMEM((N_KV, WINDOW + tm, KV_DIM), BF16),
            pltpu.VMEM((N_KV, WINDOW + tm, KV_DIM), BF16),
            pltpu.VMEM((N_CT, HIST_U + tm, LANES), F32),
            pltpu.VMEM((tm, CONV_DIM), F32),
            pltpu.VMEM((tm, ATTN_DIM), BF16),
            pltpu.VMEM((tm, CONV_DIM), BF16),
        ],
        compiler_params=pltpu.CompilerParams(
            dimension_semantics=("arbitrary", "arbitrary"), vmem_limit_bytes=VMEM_LIMIT),
        name="mix_prompt",
    )(sinks, x, gate, zo, zo, zo, zo, zo, kv, kv, zo, *weights)


def _mix_sample_call(sinks, x, gate, zo, kv, kv_hist, u_hist, weights, *, nb):
    b, t_rows, _ = x.shape
    tm = nb * t_rows
    in_specs = [
        pl.BlockSpec(memory_space=pltpu.SMEM),
        pl.BlockSpec((nb, t_rows, D_MODEL), lambda i: (i, 0, 0)),
        pl.BlockSpec((nb, 1, D_MODEL), lambda i: (i, 0, 0)),
        pl.BlockSpec((tm, 1024), lambda i: (i, 0)),
        pl.BlockSpec((tm, 1024), lambda i: (i, 1)),
        pl.BlockSpec((tm, 1024), lambda i: (i, 2)),
        pl.BlockSpec((tm, 1024), lambda i: (i, 3)),
        pl.BlockSpec((tm, 4096), lambda i: (i, 1)),
        pl.BlockSpec((tm, 2 * KV_DIM), lambda i: (i, 0)),
        pl.BlockSpec((nb, WINDOW, 2 * KV_DIM), lambda i: (i, 0, 0)),
        pl.BlockSpec((nb, HIST_U, CONV_DIM), lambda i: (i, 0, 0)),
    ] + _weight_specs()
    return pl.pallas_call(
        _mix_sample_kernel,
        grid=(b // nb,),
        in_specs=in_specs,
        out_specs=pl.BlockSpec((nb, t_rows, D_MODEL), lambda i: (i, 0, 0)),
        out_shape=jax.ShapeDtypeStruct(x.shape, F32),
        scratch_shapes=[
            pltpu.VMEM((N_CT, HIST_U + t_rows, LANES), F32),
            pltpu.VMEM((tm, CONV_DIM), F32),
            pltpu.VMEM((tm, ATTN_DIM), BF16),
            pltpu.VMEM((tm, CONV_DIM), BF16),
        ],
        compiler_params=pltpu.CompilerParams(
            dimension_semantics=("arbitrary",), vmem_limit_bytes=VMEM_LIMIT),
        name="mix_sample",
    )(sinks, x, gate, zo, zo, zo, zo, zo, kv, kv_hist, u_hist, *weights)


def _q_cols(w):
    lead = w.shape[:-1]
    w = w.reshape(*lead, N_KV, GROUP, 2, HALF)
    return jnp.moveaxis(w, (-4, -3, -2), (-2, -4, -3)).reshape(*lead, ATTN_DIM)


def _k_cols(w):
    lead = w.shape[:-1]
    w = w.reshape(*lead, N_KV, 2, HALF)
    return jnp.swapaxes(w, -3, -2).reshape(*lead, KV_DIM)


def _k_cols_inv(w):
    lead = w.shape[:-1]
    w = w.reshape(*lead, 2, N_KV, HALF)
    return jnp.swapaxes(w, -3, -2).reshape(*lead, KV_DIM)


def _a_cols(w):
    lead = w.shape[:-1]
    w = w.reshape(*lead, N_KV, GROUP, HEAD_DIM)
    return jnp.swapaxes(w, -3, -2).reshape(*lead, ATTN_DIM)


def _rope_tables(pos):
    inv = ROPE_THETA ** (-2.0 * jnp.arange(HALF, dtype=F32) / HEAD_DIM)
    ang = pos.astype(F32)[:, None] * inv[None, :]
    return jnp.tile(jnp.cos(ang), (1, N_KV)), jnp.tile(jnp.sin(ang), (1, N_KV))


def _layer(l, xp, xs, c_prompt, c_sample, cache_k, cache_v, state_conv, norm_g, w_ada, b_ada, w_in,
           sinks, w_dw, b_dw, ln_g, ln_b, w_proj_a, w_proj_b, w_out, final_g):
    b, s, _ = xp.shape
    db, ds, _ = xs.shape
    n_win = cache_k.shape[2]
    in_tm = min(s, 1024)
    assert n_win == WINDOW and s % in_tm == 0 and s % 256 == 0 and db % 8 == 0 and ds % 16 == 0

    wi = w_in[l]
    w_head = jnp.concatenate([_q_cols(wi[:, :Q_END]), _a_cols(wi[:, V_END:GA_END])], axis=1).astype(BF16)
    w_rest = wi[:, GA_END:].astype(BF16)
    w_kv = jnp.concatenate([_k_cols(wi[:, Q_END:K_END]), wi[:, K_END:V_END]], axis=1).astype(BF16)
    wa = jnp.swapaxes(w_proj_a[l].reshape(N_KV, GROUP, HEAD_DIM, D_MODEL), 0, 1)
    wa = wa.reshape(ATTN_DIM, D_MODEL).astype(BF16)
    wb = w_proj_b[l].astype(BF16)
    wo = w_out[l].astype(BF16)
    wdw = jnp.pad(w_dw[l], ((0, HIST_U - CONV_WIDTH), (0, 0)))
    weights = (wdw, b_dw[l][None], ln_g[l][None], ln_b[l][None], wa, wb, wo, final_g[None, :])

    pad_rows = (-(b + db)) % 16
    c_all = jnp.concatenate([c_prompt, c_sample, jnp.zeros((pad_rows, D_MODEL), F32)], axis=0)
    mod = _mod_call(c_all, w_ada[l], b_ada[l][None])
    shift, scale, gate = (mod[:, i * D_MODEL:(i + 1) * D_MODEL][:, None, :] for i in range(3))

    cos_p, sin_p = _rope_tables(jnp.arange(s))
    cos_s, sin_s = _rope_tables(PAST_LEN + jnp.arange(ds))
    cos_s, sin_s = jnp.tile(cos_s, (db, 1)), jnp.tile(sin_s, (db, 1))

    zo_p, kv_p = _inproj_call(xp, shift[:b], scale[:b], norm_g[l][None], cos_p, sin_p,
                              w_head, w_rest, w_kv, nb=1, t_rows=in_tm)
    yp = _mix_prompt_call(sinks[l], xp, gate[:b], zo_p, kv_p, weights, tm=256)
    kv3 = kv_p.reshape(b, s, 2 * KV_DIM)[:, s - n_win:].astype(F32)
    new_k_p = _k_cols_inv(kv3[..., :KV_DIM]).reshape(b, n_win, N_KV, HEAD_DIM)
    new_v_p = kv3[..., KV_DIM:].reshape(b, n_win, N_KV, HEAD_DIM)
    new_c_p = zo_p.reshape(b, s, ZO_COLS)[:, s - (CONV_WIDTH - 1):, 2048:3072].astype(F32)

    zo_s, kv_s = _inproj_call(xs, shift[b:b + db], scale[b:b + db], norm_g[l][None], cos_s, sin_s,
                              w_head, w_rest, w_kv, nb=db, t_rows=ds)
    kv_hist = jnp.concatenate(
        [_k_cols(cache_k[l].reshape(db, n_win, KV_DIM)), cache_v[l].reshape(db, n_win, KV_DIM)],
        axis=-1).astype(BF16)
    u_hist = jnp.pad(state_conv[l], ((0, 0), (HIST_U - (CONV_WIDTH - 1), 0), (0, 0)))
    ys = _mix_sample_call(sinks[l], xs, gate[b:b + db], zo_s, kv_s, kv_hist, u_hist, weights, nb=8)
    kvs3 = kv_s.reshape(db, ds, 2 * KV_DIM).astype(F32)
    k_new = _k_cols_inv(kvs3[..., :KV_DIM]).reshape(db, ds, N_KV, HEAD_DIM)
    v_new = kvs3[..., KV_DIM:].reshape(db, ds, N_KV, HEAD_DIM)
    new_k_s = jnp.concatenate([cache_k[l], k_new], axis=1)[:, -n_win:]
    new_v_s = jnp.concatenate([cache_v[l], v_new], axis=1)[:, -n_win:]
    u_new = zo_s.reshape(db, ds, ZO_COLS)[:, :, 2048:3072].astype(F32)
    new_c_s = jnp.concatenate([state_conv[l], u_new], axis=1)[:, -(CONV_WIDTH - 1):]
    return yp, ys, new_k_p, new_v_p, new_c_p, new_k_s, new_v_s, new_c_s


def kernel(x_prompt, x_sample, c_prompt, c_sample, cache_k, cache_v, state_conv, norm_g, w_ada, b_ada,
           w_in, sinks, w_dw, b_dw, ln_g, ln_b, w_proj_a, w_proj_b, w_out, final_g):
    assert w_in.shape[0] == 1
    res = _layer(0, x_prompt, x_sample, c_prompt, c_sample, cache_k, cache_v, state_conv, norm_g, w_ada,
                 b_ada, w_in, sinks, w_dw, b_dw, ln_g, ln_b, w_proj_a, w_proj_b, w_out, final_g)
    return res[:2] + tuple(r[None] for r in res[2:])
```

```python
import jax
import jax.numpy as jnp
from jax import lax
from jax.experimental import pallas as pl
from jax.experimental.pallas import tpu as pltpu

D_MODEL = 2048
N_HEADS = 16
N_KV = 4
HEAD_DIM = 64
HALF = HEAD_DIM // 2
GROUP = N_HEADS // N_KV
ATTN_DIM = N_HEADS * HEAD_DIM
KV_DIM = N_KV * HEAD_DIM
CONV_DIM = 1024
CONV_WIDTH = 31
CHUNK = 64
WINDOW = 128
ROPE_THETA = 10000.0
RMS_EPS = 1e-6
LN_EPS = 1e-5
NEG_INF = -1e30
PAST_LEN = 1024

Q_END = ATTN_DIM
K_END = Q_END + KV_DIM
V_END = K_END + KV_DIM
GA_END = V_END + ATTN_DIM
CU_END = GA_END + 2 * CONV_DIM
GB_END = CU_END + CONV_DIM
IN_DIM = GB_END + 2 * D_MODEL

LANES = 128
HIST_U = 32
N_CT = CONV_DIM // LANES
VMEM_LIMIT = 56 * 1024 * 1024

N_ZO_BLK = 8
WBLK = 1024
N_HEAD_BLK = 2
N_WBLK = 9
ROW_SUB = 256
NORM_ROWS = 128

BF16 = jnp.bfloat16
F32 = jnp.float32


def _sigmoid(x):
    return 1.0 / (1.0 + jnp.exp(-x))


def _mod_kernel(c_ref, w_ref, b_ref, o_ref):
    c = c_ref[...]
    a = (c * _sigmoid(c)).astype(BF16)
    w = w_ref[...].astype(BF16)
    o_ref[...] = jnp.dot(a, w, preferred_element_type=F32) + b_ref[...]


def _mod_call(c_all, w_ada, b_ada):
    rows = c_all.shape[0]
    n = w_ada.shape[1]
    tn = 1024
    return pl.pallas_call(
        _mod_kernel,
        grid=(n // tn,),
        in_specs=[
            pl.BlockSpec((rows, D_MODEL), lambda j: (0, 0)),
            pl.BlockSpec((D_MODEL, tn), lambda j: (0, j)),
            pl.BlockSpec((1, tn), lambda j: (0, j)),
        ],
        out_specs=pl.BlockSpec((rows, tn), lambda j: (0, j)),
        out_shape=jax.ShapeDtypeStruct((rows, n), F32),
        compiler_params=pltpu.CompilerParams(
            dimension_semantics=("arbitrary",), vmem_limit_bytes=VMEM_LIMIT),
        name="adaln_mod",
    )(c_all, w_ada, b_ada)


def _rope_pair(a, b, cos, sin):
    return a * cos - b * sin, b * cos + a * sin


def _inproj_kernel(x_ref, shift_ref, scale_ref, g_ref, cos_ref, sin_ref, wh_ref, wr_ref, wkv_ref,
                   zo_ref, kv_ref, h_scr, glu_scr):
    j = pl.program_id(1)
    nb, t_rows, _ = x_ref.shape
    tm = nb * t_rows
    sub = min(tm, ROW_SUB)
    row_blocks = range(0, tm, sub)

    def proj(w_ref, r0):
        return jnp.dot(h_scr[r0:r0 + sub, :], w_ref[...], preferred_element_type=F32)

    def norm_rows(r0, rc):
        if rc <= t_rows:
            bi, t0 = divmod(r0, t_rows)
            x = x_ref[bi, t0:t0 + rc, :]
            scale, shift = scale_ref[bi], shift_ref[bi]
        else:
            b0, nbc = r0 // t_rows, rc // t_rows
            x = x_ref[b0:b0 + nbc]
            scale, shift = scale_ref[b0:b0 + nbc], shift_ref[b0:b0 + nbc]
        ms = jnp.mean(x * x, axis=-1, keepdims=True)
        y = x * lax.rsqrt(ms + RMS_EPS) * g_ref[...]
        h = y * (1.0 + scale) + shift
        h_scr[r0:r0 + rc, :] = h.reshape(rc, D_MODEL).astype(BF16)

    @pl.when(j == 0)
    def _():
        rc = min(sub, NORM_ROWS)
        for r0 in row_blocks:
            for c0 in range(r0, r0 + sub, rc):
                norm_rows(c0, rc)
            rows = slice(r0, r0 + sub)
            cos = cos_ref[rows, :]
            sin = sin_ref[rows, :]
            acc = jnp.dot(h_scr[rows, :], wkv_ref[...], preferred_element_type=F32)
            ka, kb = _rope_pair(acc[:, 0:LANES], acc[:, LANES:2 * LANES], cos, sin)
            kv_ref[rows, 0:LANES] = ka.astype(BF16)
            kv_ref[rows, LANES:2 * LANES] = kb.astype(BF16)
            kv_ref[rows, 2 * LANES:] = acc[:, 2 * LANES:].astype(BF16)
            acc = proj(wh_ref, r0)
            for i in range(GROUP):
                lo = i * 2 * LANES
                qa, qb = _rope_pair(acc[:, lo:lo + LANES], acc[:, lo + LANES:lo + 2 * LANES], cos, sin)
                zo_ref[rows, lo:lo + LANES] = qa.astype(BF16)
                zo_ref[rows, lo + LANES:lo + 2 * LANES] = qb.astype(BF16)

    @pl.when(j == 1)
    def _():
        for r0 in row_blocks:
            acc = proj(wh_ref, r0)
            zo_ref[r0:r0 + sub, :] = (acc * _sigmoid(acc)).astype(BF16)

    @pl.when(j == 2)
    def _():
        for r0 in row_blocks:
            glu_scr[r0:r0 + sub, :] = proj(wr_ref, r0)

    @pl.when(j == 3)
    def _():
        for r0 in row_blocks:
            zo_ref[r0:r0 + sub, :] = (glu_scr[r0:r0 + sub, :] * _sigmoid(proj(wr_ref, r0))).astype(BF16)

    @pl.when(j == 4)
    def _():
        for r0 in row_blocks:
            acc = proj(wr_ref, r0)
            zo_ref[r0:r0 + sub, :] = (acc * _sigmoid(acc)).astype(BF16)

    @pl.when(j >= 5)
    def _():
        for r0 in row_blocks:
            zo_ref[r0:r0 + sub, :] = _sigmoid(proj(wr_ref, r0)).astype(BF16)


def _inproj_call(x, shift, scale, norm_g, cos_t, sin_t, w_head, w_rest, w_kv, *, nb, t_rows):
    b, s, _ = x.shape
    tm = nb * t_rows
    tiles_per_batch = s // t_rows
    n_tiles = (b // nb) * tiles_per_batch
    pos_tiles = cos_t.shape[0] // tm

    def x_map(i, j):
        return (i // tiles_per_batch, i % tiles_per_batch, 0)

    def mod_map(i, j):
        return (i // tiles_per_batch, 0, 0)

    def pos_map(i, j):
        return (i % pos_tiles, 0)

    def zo_map(i, j):
        return (jnp.where(j < 3, j, j - 1), i, 0)

    return pl.pallas_call(
        _inproj_kernel,
        grid=(n_tiles, N_WBLK),
        in_specs=[
            pl.BlockSpec((nb, t_rows, D_MODEL), x_map),
            pl.BlockSpec((nb, 1, D_MODEL), mod_map),
            pl.BlockSpec((nb, 1, D_MODEL), mod_map),
            pl.BlockSpec((1, D_MODEL), lambda i, j: (0, 0)),
            pl.BlockSpec((tm, LANES), pos_map),
            pl.BlockSpec((tm, LANES), pos_map),
            pl.BlockSpec((None, D_MODEL, WBLK), lambda i, j: (jnp.minimum(j, N_HEAD_BLK - 1), 0, 0)),
            pl.BlockSpec((None, D_MODEL, WBLK), lambda i, j: (jnp.maximum(j - N_HEAD_BLK, 0), 0, 0)),
            pl.BlockSpec((D_MODEL, 2 * KV_DIM), lambda i, j: (0, 0)),
        ],
        out_specs=[
            pl.BlockSpec((None, tm, WBLK), zo_map),
            pl.BlockSpec((tm, 2 * KV_DIM), lambda i, j: (i, 0)),
        ],
        out_shape=[
            jax.ShapeDtypeStruct((N_ZO_BLK, b * s, WBLK), BF16),
            jax.ShapeDtypeStruct((b * s, 2 * KV_DIM), BF16),
        ],
        scratch_shapes=[
            pltpu.VMEM((tm, D_MODEL), BF16),
            pltpu.VMEM((tm, WBLK), F32),
        ],
        compiler_params=pltpu.CompilerParams(
            dimension_semantics=("arbitrary", "arbitrary"), vmem_limit_bytes=VMEM_LIMIT),
        name="in_proj",
    )(x, shift, scale, norm_g, cos_t, sin_t, w_head, w_rest, w_kv)


def _kv_group_masks():
    lane = lax.broadcasted_iota(jnp.int32, (1, 2 * LANES), 1)
    kmask = [((lane % LANES) // HALF) == g for g in range(N_KV)]
    vmask = [(lane // HEAD_DIM) == g for g in range(N_KV)]
    return kmask, vmask


def _attend_chunk(qs, kwin, vwin, sinks_ref, invalid):
    rows = qs.shape[0]
    cq = rows // GROUP
    rid = lax.broadcasted_iota(jnp.int32, (rows, 1), 0)
    out = None
    for g in range(N_KV):
        s = lax.dot_general(qs, kwin[g], (((1,), (1,)), ((), ())), preferred_element_type=F32)
        s = s * (HEAD_DIM ** -0.5)
        if invalid is not None:
            s = jnp.where(invalid, NEG_INF, s)
        sink = jnp.full((rows, 1), sinks_ref[GROUP * g + GROUP - 1], F32)
        for i in range(GROUP - 2, -1, -1):
            sink = jnp.where(rid < (i + 1) * cq, sinks_ref[GROUP * g + i], sink)
        m = jnp.maximum(jnp.max(s, axis=-1, keepdims=True), sink)
        e = jnp.exp(s - m)
        denom = jnp.sum(e, axis=-1, keepdims=True) + jnp.exp(sink - m)
        p = (e / denom).astype(BF16)
        o = jnp.dot(p, vwin[g], preferred_element_type=F32)
        out = o if out is None else out + o
    return out


def _conv_ln_swish(ubuf, wdw_ref, bdw_ref, lng_ref, lnb_ref, y_scr, row0, n_rows):
    first = HIST_U - (CONV_WIDTH - 1)
    for ct in range(N_CT):
        cs = slice(ct * LANES, (ct + 1) * LANES)
        acc = jnp.broadcast_to(bdw_ref[:, cs], (n_rows, LANES))
        for k in range(CONV_WIDTH):
            acc = acc + wdw_ref[k:k + 1, cs] * ubuf[ct, pl.ds(first + k, n_rows), :]
        y_scr[pl.ds(row0, n_rows), cs] = acc
    y = y_scr[pl.ds(row0, n_rows), :]
    mu = jnp.mean(y, axis=-1, keepdims=True)
    yc = y - mu
    var = jnp.mean(yc * yc, axis=-1, keepdims=True)
    yn = yc * lax.rsqrt(var + LN_EPS) * lng_ref[...] + lnb_ref[...]
    return yn * _sigmoid(yn)


def _out_tail(x, gate, ao, co, smg_ref, wa_ref, wb_ref, wo_ref, fg_ref):
    pa = jnp.dot(ao, wa_ref[...], preferred_element_type=F32)
    pb = jnp.dot(co, wb_ref[...], preferred_element_type=F32)
    sa = jnp.concatenate([smg_ref[0], smg_ref[1]], axis=1).astype(F32)
    sb = jnp.concatenate([smg_ref[2], smg_ref[3]], axis=1).astype(F32)
    merged = sa * pa + sb * pb
    o = jnp.dot(merged.astype(BF16), wo_ref[...], preferred_element_type=F32)
    r = x + gate * o
    ms = jnp.mean(r * r, axis=-1, keepdims=True)
    return r * lax.rsqrt(ms + RMS_EPS) * fg_ref[...]


def _mix_prompt_kernel(sinks_ref, x_ref, gate_ref, q_ref, sga_ref, u_ref, sgb_ref, smg_ref,
                       kv_ref, kvh_ref, uh_ref, wdw_ref, bdw_ref, lng_ref, lnb_ref,
                       wa_ref, wb_ref, wo_ref, fg_ref, y_ref,
                       km_scr, vm_scr, ubuf, cy_scr, ao_scr, co_scr):
    t = pl.program_id(1)
    tm = q_ref.shape[0]
    n_chunks = tm // CHUNK
    first_tile = t == 0

    kmask, vmask = _kv_group_masks()
    kh = kvh_ref[:, 0:KV_DIM]
    vh = kvh_ref[:, KV_DIM:]
    vh = jnp.where(first_tile, jnp.zeros_like(vh), vh)
    kc = kv_ref[:, 0:KV_DIM]
    vc = kv_ref[:, KV_DIM:]
    zero = jnp.zeros((), BF16)
    for g in range(N_KV):
        km_scr[g, 0:WINDOW, :] = jnp.where(kmask[g], kh, zero)
        km_scr[g, WINDOW:, :] = jnp.where(kmask[g], kc, zero)
        vm_scr[g, 0:WINDOW, :] = jnp.where(vmask[g], vh, zero)
        vm_scr[g, WINDOW:, :] = jnp.where(vmask[g], vc, zero)

    nk = WINDOW + CHUNK
    key_id = lax.broadcasted_iota(jnp.int32, (1, nk), 1)
    for c in range(n_chunks):
        r0 = c * CHUNK
        qs = jnp.concatenate(
            [q_ref[r0:r0 + CHUNK, i * 2 * LANES:(i + 1) * 2 * LANES] for i in range(GROUP)], axis=0)
        kwin = [km_scr[g, r0:r0 + nk, :] for g in range(N_KV)]
        vwin = [vm_scr[g, r0:r0 + nk, :] for g in range(N_KV)]
        n_bad = WINDOW - r0
        invalid = (first_tile & (key_id < n_bad)) if n_bad > 0 else None
        o = _attend_chunk(qs, kwin, vwin, sinks_ref, invalid)
        for i in range(GROUP):
            cs = slice(i * 2 * LANES, (i + 1) * 2 * LANES)
            ao_scr[r0:r0 + CHUNK, cs] = (
                o[i * CHUNK:(i + 1) * CHUNK, :] * sga_ref[r0:r0 + CHUNK, cs].astype(F32)).astype(BF16)

    uh = uh_ref[...].astype(F32)
    uh = jnp.where(first_tile, jnp.zeros_like(uh), uh)
    for ct in range(N_CT):
        cs = slice(ct * LANES, (ct + 1) * LANES)
        ubuf[ct, 0:HIST_U, :] = uh[:, cs]
        ubuf[ct, HIST_U:, :] = u_ref[:, cs].astype(F32)
    act = _conv_ln_swish(ubuf, wdw_ref, bdw_ref, lng_ref, lnb_ref, cy_scr, 0, tm)
    co_scr[...] = (act * sgb_ref[...].astype(F32)).astype(BF16)

    y_ref[0] = _out_tail(x_ref[0], gate_ref[0], ao_scr[...], co_scr[...], smg_ref,
                         wa_ref, wb_ref, wo_ref, fg_ref)


def _mix_sample_kernel(sinks_ref, x_ref, gate_ref, q_ref, sga_ref, u_ref, sgb_ref, smg_ref,
                       kv_ref, kvh_ref, uh_ref, wdw_ref, bdw_ref, lng_ref, lnb_ref,
                       wa_ref, wb_ref, wo_ref, fg_ref, y_ref,
                       ubuf, cy_scr, ao_scr, co_scr):
    nb, t_rows, _ = x_ref.shape
    kmask, vmask = _kv_group_masks()
    zero = jnp.zeros((), BF16)
    for b in range(nb):
        r0 = b * t_rows
        kall = jnp.concatenate([kvh_ref[b, :, 0:KV_DIM], kv_ref[r0:r0 + t_rows, 0:KV_DIM]], axis=0)
        vall = jnp.concatenate([kvh_ref[b, :, KV_DIM:], kv_ref[r0:r0 + t_rows, KV_DIM:]], axis=0)
        kwin = [jnp.where(kmask[g], kall, zero) for g in range(N_KV)]
        vwin = [jnp.where(vmask[g], vall, zero) for g in range(N_KV)]
        qs = jnp.concatenate(
            [q_ref[r0:r0 + t_rows, i * 2 * LANES:(i + 1) * 2 * LANES] for i in range(GROUP)], axis=0)
        o = _attend_chunk(qs, kwin, vwin, sinks_ref, None)
        for i in range(GROUP):
            cs = slice(i * 2 * LANES, (i + 1) * 2 * LANES)
            ao_scr[r0:r0 + t_rows, cs] = (
                o[i * t_rows:(i + 1) * t_rows, :] * sga_ref[r0:r0 + t_rows, cs].astype(F32)).astype(BF16)

        for ct in range(N_CT):
            cs = slice(ct * LANES, (ct + 1) * LANES)
            ubuf[ct, 0:HIST_U, :] = uh_ref[b, :, cs]
            ubuf[ct, HIST_U:, :] = u_ref[r0:r0 + t_rows, cs].astype(F32)
        act = _conv_ln_swish(ubuf, wdw_ref, bdw_ref, lng_ref, lnb_ref, cy_scr, r0, t_rows)
        co_scr[r0:r0 + t_rows, :] = (act * sgb_ref[r0:r0 + t_rows, :].astype(F32)).astype(BF16)

    x = x_ref[...]
    out = _out_tail(x.reshape(nb * t_rows, D_MODEL),
                    jnp.broadcast_to(gate_ref[...], x.shape).reshape(nb * t_rows, D_MODEL),
                    ao_scr[...], co_scr[...], smg_ref, wa_ref, wb_ref, wo_ref, fg_ref)
    y_ref[...] = out.reshape(x.shape)


def _const_spec(shape):
    nd = len(shape)
    return pl.BlockSpec(shape, lambda *_: (0,) * nd, pipeline_mode=pl.Buffered(1))


def _weight_specs():
    return [
        _const_spec((HIST_U, CONV_DIM)),
        _const_spec((1, CONV_DIM)),
        _const_spec((1, CONV_DIM)),
        _const_spec((1, CONV_DIM)),
        _const_spec((ATTN_DIM, D_MODEL)),
        _const_spec((CONV_DIM, D_MODEL)),
        _const_spec((D_MODEL, D_MODEL)),
        _const_spec((1, D_MODEL)),
    ]


def _mix_prompt_call(sinks, x, gate, zo, kv, weights, *, tm):
    b, s, _ = x.shape
    tpb = s // tm

    def tile(bi, t):
        return bi * tpb + t

    def kvh_map(bi, t):
        return (jnp.maximum(tile(bi, t) * (tm // WINDOW) - 1, 0), 0)

    def uh_map(bi, t):
        return (2, jnp.maximum(tile(bi, t) * (tm // HIST_U) - 1, 0), 0)

    in_specs = [
        pl.BlockSpec(memory_space=pltpu.SMEM),
        pl.BlockSpec((1, tm, D_MODEL), lambda bi, t: (bi, t, 0)),
        pl.BlockSpec((1, 1, D_MODEL), lambda bi, t: (bi, 0, 0)),
        pl.BlockSpec((None, tm, WBLK), lambda bi, t: (0, tile(bi, t), 0)),
        pl.BlockSpec((None, tm, WBLK), lambda bi, t: (1, tile(bi, t), 0)),
        pl.BlockSpec((None, tm, WBLK), lambda bi, t: (2, tile(bi, t), 0)),
        pl.BlockSpec((None, tm, WBLK), lambda bi, t: (3, tile(bi, t), 0)),
        pl.BlockSpec((4, tm, WBLK), lambda bi, t: (1, tile(bi, t), 0)),
        pl.BlockSpec((tm, 2 * KV_DIM), lambda bi, t: (tile(bi, t), 0)),
        pl.BlockSpec((WINDOW, 2 * KV_DIM), kvh_map),
        pl.BlockSpec((None, HIST_U, WBLK), uh_map),
    ] + _weight_specs()
    return pl.pallas_call(
        _mix_prompt_kernel,
        grid=(b, tpb),
        in_specs=in_specs,
        out_specs=pl.BlockSpec((1, tm, D_MODEL), lambda bi, t: (bi, t, 0)),
        out_shape=jax.ShapeDtypeStruct(x.shape, F32),
        scratch_shapes=[
            pltpu.VMEM((N_KV, WINDOW + tm, KV_DIM), BF16),
            pltpu.VMEM((N_KV, WINDOW + tm, KV_DIM), BF16),
            pltpu.VMEM((N_CT, HIST_U + tm, LANES), F32),
            pltpu.VMEM((tm, CONV_DIM), F32),
            pltpu.VMEM((tm, ATTN_DIM), BF16),
            pltpu.VMEM((tm, CONV_DIM), BF16),
        ],
        compiler_params=pltpu.CompilerParams(
            dimension_semantics=("arbitrary", "arbitrary"), vmem_limit_bytes=VMEM_LIMIT),
        name="mix_prompt",
    )(sinks, x, gate, zo, zo, zo, zo, zo, kv, kv, zo, *weights)


def _mix_sample_call(sinks, x, gate, zo, kv, kv_hist, u_hist, weights, *, nb):
    b, t_rows, _ = x.shape
    tm = nb * t_rows
    in_specs = [
        pl.BlockSpec(memory_space=pltpu.SMEM),
        pl.BlockSpec((nb, t_rows, D_MODEL), lambda i: (i, 0, 0)),
        pl.BlockSpec((nb, 1, D_MODEL), lambda i: (i, 0, 0)),
        pl.BlockSpec((None, tm, WBLK), lambda i: (0, i, 0)),
        pl.BlockSpec((None, tm, WBLK), lambda i: (1, i, 0)),
        pl.BlockSpec((None, tm, WBLK), lambda i: (2, i, 0)),
        pl.BlockSpec((None, tm, WBLK), lambda i: (3, i, 0)),
        pl.BlockSpec((4, tm, WBLK), lambda i: (1, i, 0)),
        pl.BlockSpec((tm, 2 * KV_DIM), lambda i: (i, 0)),
        pl.BlockSpec((nb, WINDOW, 2 * KV_DIM), lambda i: (i, 0, 0)),
        pl.BlockSpec((nb, HIST_U, CONV_DIM), lambda i: (i, 0, 0)),
    ] + _weight_specs()
    return pl.pallas_call(
        _mix_sample_kernel,
        grid=(b // nb,),
        in_specs=in_specs,
        out_specs=pl.BlockSpec((nb, t_rows, D_MODEL), lambda i: (i, 0, 0)),
        out_shape=jax.ShapeDtypeStruct(x.shape, F32),
        scratch_shapes=[
            pltpu.VMEM((N_CT, HIST_U + t_rows, LANES), F32),
            pltpu.VMEM((tm, CONV_DIM), F32),
            pltpu.VMEM((tm, ATTN_DIM), BF16),
            pltpu.VMEM((tm, CONV_DIM), BF16),
        ],
        compiler_params=pltpu.CompilerParams(
            dimension_semantics=("arbitrary",), vmem_limit_bytes=VMEM_LIMIT),
        name="mix_sample",
    )(sinks, x, gate, zo, zo, zo, zo, zo, kv, kv_hist, u_hist, *weights)


def _q_cols(w):
    lead = w.shape[:-1]
    w = w.reshape(*lead, N_KV, GROUP, 2, HALF)
    return jnp.moveaxis(w, (-4, -3, -2), (-2, -4, -3)).reshape(*lead, ATTN_DIM)


def _k_cols(w):
    lead = w.shape[:-1]
    w = w.reshape(*lead, N_KV, 2, HALF)
    return jnp.swapaxes(w, -3, -2).reshape(*lead, KV_DIM)


def _k_cols_inv(w):
    lead = w.shape[:-1]
    w = w.reshape(*lead, 2, N_KV, HALF)
    return jnp.swapaxes(w, -3, -2).reshape(*lead, KV_DIM)


def _a_cols(w):
    lead = w.shape[:-1]
    w = w.reshape(*lead, N_KV, GROUP, HEAD_DIM)
    return jnp.swapaxes(w, -3, -2).reshape(*lead, ATTN_DIM)


def _rope_tables(pos):
    inv = ROPE_THETA ** (-2.0 * jnp.arange(HALF, dtype=F32) / HEAD_DIM)
    ang = pos.astype(F32)[:, None] * inv[None, :]
    return jnp.tile(jnp.cos(ang), (1, N_KV)), jnp.tile(jnp.sin(ang), (1, N_KV))


def _layer(l, xp, xs, c_prompt, c_sample, cache_k, cache_v, state_conv, norm_g, w_ada, b_ada, w_in,
           sinks, w_dw, b_dw, ln_g, ln_b, w_proj_a, w_proj_b, w_out, final_g):
    b, s, _ = xp.shape
    db, ds, _ = xs.shape
    n_win = cache_k.shape[2]
    in_tm = min(s, 1024)
    assert n_win == WINDOW and s % in_tm == 0 and s % 256 == 0 and db % 8 == 0 and ds % 16 == 0

    wi = w_in[l]
    w_head = jnp.stack([_q_cols(wi[:, :Q_END]), _a_cols(wi[:, V_END:GA_END])]).astype(BF16)
    w_rest = jnp.swapaxes(wi[:, GA_END:].reshape(D_MODEL, N_WBLK - N_HEAD_BLK, WBLK), 0, 1).astype(BF16)
    w_kv = jnp.concatenate([_k_cols(wi[:, Q_END:K_END]), wi[:, K_END:V_END]], axis=1).astype(BF16)
    wa = jnp.swapaxes(w_proj_a[l].reshape(N_KV, GROUP, HEAD_DIM, D_MODEL), 0, 1)
    wa = wa.reshape(ATTN_DIM, D_MODEL).astype(BF16)
    wb = w_proj_b[l].astype(BF16)
    wo = w_out[l].astype(BF16)
    wdw = jnp.pad(w_dw[l], ((0, HIST_U - CONV_WIDTH), (0, 0)))
    weights = (wdw, b_dw[l][None], ln_g[l][None], ln_b[l][None], wa, wb, wo, final_g[None, :])

    pad_rows = (-(b + db)) % 16
    c_all = jnp.concatenate([c_prompt, c_sample, jnp.zeros((pad_rows, D_MODEL), F32)], axis=0)
    mod = _mod_call(c_all, w_ada[l], b_ada[l][None])
    shift, scale, gate = (mod[:, i * D_MODEL:(i + 1) * D_MODEL][:, None, :] for i in range(3))

    cos_p, sin_p = _rope_tables(jnp.arange(s))
    cos_s, sin_s = _rope_tables(PAST_LEN + jnp.arange(ds))
    cos_s, sin_s = jnp.tile(cos_s, (db, 1)), jnp.tile(sin_s, (db, 1))

    zo_p, kv_p = _inproj_call(xp, shift[:b], scale[:b], norm_g[l][None], cos_p, sin_p,
                              w_head, w_rest, w_kv, nb=1, t_rows=in_tm)
    yp = _mix_prompt_call(sinks[l], xp, gate[:b], zo_p, kv_p, weights, tm=256)
    kv3 = kv_p.reshape(b, s, 2 * KV_DIM)[:, s - n_win:].astype(F32)
    new_k_p = _k_cols_inv(kv3[..., :KV_DIM]).reshape(b, n_win, N_KV, HEAD_DIM)
    new_v_p = kv3[..., KV_DIM:].reshape(b, n_win, N_KV, HEAD_DIM)
    new_c_p = zo_p[2].reshape(b, s, CONV_DIM)[:, s - (CONV_WIDTH - 1):].astype(F32)

    zo_s, kv_s = _inproj_call(xs, shift[b:b + db], scale[b:b + db], norm_g[l][None], cos_s, sin_s,
                              w_head, w_rest, w_kv, nb=db, t_rows=ds)
    kv_hist = jnp.concatenate(
        [_k_cols(cache_k[l].reshape(db, n_win, KV_DIM)), cache_v[l].reshape(db, n_win, KV_DIM)],
        axis=-1).astype(BF16)
    u_hist = jnp.pad(state_conv[l], ((0, 0), (HIST_U - (CONV_WIDTH - 1), 0), (0, 0)))
    ys = _mix_sample_call(sinks[l], xs, gate[b:b + db], zo_s, kv_s, kv_hist, u_hist, weights, nb=8)
    kvs3 = kv_s.reshape(db, ds, 2 * KV_DIM).astype(F32)
    k_new = _k_cols_inv(kvs3[..., :KV_DIM]).reshape(db, ds, N_KV, HEAD_DIM)
    v_new = kvs3[..., KV_DIM:].reshape(db, ds, N_KV, HEAD_DIM)
    new_k_s = jnp.concatenate([cache_k[l], k_new], axis=1)[:, -n_win:]
    new_v_s = jnp.concatenate([cache_v[l], v_new], axis=1)[:, -n_win:]
    u_new = zo_s[2].reshape(db, ds, CONV_DIM).astype(F32)
    new_c_s = jnp.concatenate([state_conv[l], u_new], axis=1)[:, -(CONV_WIDTH - 1):]
    return yp, ys, new_k_p, new_v_p, new_c_p, new_k_s, new_v_s, new_c_s


def kernel(x_prompt, x_sample, c_prompt, c_sample, cache_k, cache_v, state_conv, norm_g, w_ada, b_ada,
           w_in, sinks, w_dw, b_dw, ln_g, ln_b, w_proj_a, w_proj_b, w_out, final_g):
    assert w_in.shape[0] == 1
    res = _layer(0, x_prompt, x_sample, c_prompt, c_sample, cache_k, cache_v, state_conv, norm_g, w_ada,
                 b_ada, w_in, sinks, w_dw, b_dw, ln_g, ln_b, w_proj_a, w_proj_b, w_out, final_g)
    return res[:2] + tuple(r[None] for r in res[2:])
```

```python
import jax
import jax.numpy as jnp
from jax import lax
from jax.experimental import pallas as pl
from jax.experimental.pallas import tpu as pltpu

D_MODEL = 2048
N_HEADS = 16
N_KV = 4
HEAD_DIM = 64
HALF = HEAD_DIM // 2
GROUP = N_HEADS // N_KV
ATTN_DIM = N_HEADS * HEAD_DIM
KV_DIM = N_KV * HEAD_DIM
CONV_DIM = 1024
CONV_WIDTH = 31
CHUNK = 64
WINDOW = 128
ROPE_THETA = 10000.0
RMS_EPS = 1e-6
LN_EPS = 1e-5
NEG_INF = -1e30
PAST_LEN = 1024

Q_END = ATTN_DIM
K_END = Q_END + KV_DIM
V_END = K_END + KV_DIM
GA_END = V_END + ATTN_DIM
CU_END = GA_END + 2 * CONV_DIM
GB_END = CU_END + CONV_DIM
IN_DIM = GB_END + 2 * D_MODEL

LANES = 128
HIST_U = 32
N_CT = CONV_DIM // LANES
VMEM_LIMIT = 56 * 1024 * 1024

N_ZO_BLK = 8
WBLK = 1024
N_HEAD_BLK = 2
N_WBLK = 9
ROW_SUB = 256
NORM_ROWS = 128

BF16 = jnp.bfloat16
F32 = jnp.float32


def _sigmoid(x):
    return 1.0 / (1.0 + jnp.exp(-x))


def _mod_kernel(c_ref, w_ref, b_ref, o_ref):
    c = c_ref[...]
    a = (c * _sigmoid(c)).astype(BF16)
    w = w_ref[...].astype(BF16)
    o_ref[...] = jnp.dot(a, w, preferred_element_type=F32) + b_ref[...]


def _mod_call(c_all, w_ada, b_ada):
    rows = c_all.shape[0]
    n = w_ada.shape[1]
    tn = 1024
    return pl.pallas_call(
        _mod_kernel,
        grid=(n // tn,),
        in_specs=[
            pl.BlockSpec((rows, D_MODEL), lambda j: (0, 0)),
            pl.BlockSpec((D_MODEL, tn), lambda j: (0, j)),
            pl.BlockSpec((1, tn), lambda j: (0, j)),
        ],
        out_specs=pl.BlockSpec((rows, tn), lambda j: (0, j)),
        out_shape=jax.ShapeDtypeStruct((rows, n), F32),
        compiler_params=pltpu.CompilerParams(
            dimension_semantics=("arbitrary",), vmem_limit_bytes=VMEM_LIMIT),
        name="adaln_mod",
    )(c_all, w_ada, b_ada)


def _rope_pair(a, b, cos, sin):
    return a * cos - b * sin, b * cos + a * sin


def _inproj_kernel(x_ref, shift_ref, scale_ref, g_ref, cos_ref, sin_ref, wh_ref, wr_ref, wkv_ref,
                   zo_ref, kv_ref, h_scr, glu_scr):
    j = pl.program_id(1)
    nb, t_rows, _ = x_ref.shape
    tm = nb * t_rows
    sub = min(tm, ROW_SUB)
    row_blocks = range(0, tm, sub)

    def proj(w_ref, r0):
        return jnp.dot(h_scr[r0:r0 + sub, :], w_ref[...], preferred_element_type=F32)

    def norm_rows(r0, rc):
        if rc <= t_rows:
            bi, t0 = divmod(r0, t_rows)
            x = x_ref[bi, t0:t0 + rc, :]
            scale, shift = scale_ref[bi], shift_ref[bi]
        else:
            b0, nbc = r0 // t_rows, rc // t_rows
            x = x_ref[b0:b0 + nbc]
            scale, shift = scale_ref[b0:b0 + nbc], shift_ref[b0:b0 + nbc]
        ms = jnp.mean(x * x, axis=-1, keepdims=True)
        y = x * lax.rsqrt(ms + RMS_EPS) * g_ref[...]
        h = y * (1.0 + scale) + shift
        h_scr[r0:r0 + rc, :] = h.reshape(rc, D_MODEL).astype(BF16)

    @pl.when(j == 0)
    def _():
        rc = min(sub, NORM_ROWS)
        for r0 in row_blocks:
            for c0 in range(r0, r0 + sub, rc):
                norm_rows(c0, rc)
            rows = slice(r0, r0 + sub)
            cos = cos_ref[rows, :]
            sin = sin_ref[rows, :]
            acc = jnp.dot(h_scr[rows, :], wkv_ref[...], preferred_element_type=F32)
            ka, kb = _rope_pair(acc[:, 0:LANES], acc[:, LANES:2 * LANES], cos, sin)
            kv_ref[rows, 0:LANES] = ka.astype(BF16)
            kv_ref[rows, LANES:2 * LANES] = kb.astype(BF16)
            kv_ref[rows, 2 * LANES:] = acc[:, 2 * LANES:].astype(BF16)
            acc = proj(wh_ref, r0)
            for i in range(GROUP):
                lo = i * 2 * LANES
                qa, qb = _rope_pair(acc[:, lo:lo + LANES], acc[:, lo + LANES:lo + 2 * LANES], cos, sin)
                zo_ref[rows, lo:lo + LANES] = qa.astype(BF16)
                zo_ref[rows, lo + LANES:lo + 2 * LANES] = qb.astype(BF16)

    @pl.when(j == 1)
    def _():
        for r0 in row_blocks:
            acc = proj(wh_ref, r0)
            zo_ref[r0:r0 + sub, :] = (acc * _sigmoid(acc)).astype(BF16)

    @pl.when(j == 2)
    def _():
        for r0 in row_blocks:
            glu_scr[r0:r0 + sub, :] = proj(wr_ref, r0)

    @pl.when(j == 3)
    def _():
        for r0 in row_blocks:
            zo_ref[r0:r0 + sub, :] = (glu_scr[r0:r0 + sub, :] * _sigmoid(proj(wr_ref, r0))).astype(BF16)

    @pl.when(j == 4)
    def _():
        for r0 in row_blocks:
            acc = proj(wr_ref, r0)
            zo_ref[r0:r0 + sub, :] = (acc * _sigmoid(acc)).astype(BF16)

    @pl.when(j >= 5)
    def _():
        for r0 in row_blocks:
            zo_ref[r0:r0 + sub, :] = _sigmoid(proj(wr_ref, r0)).astype(BF16)


def _inproj_call(x, shift, scale, norm_g, cos_t, sin_t, w_head, w_rest, w_kv, *, nb, t_rows):
    b, s, _ = x.shape
    tm = nb * t_rows
    tiles_per_batch = s // t_rows
    n_tiles = (b // nb) * tiles_per_batch
    pos_tiles = cos_t.shape[0] // tm

    def x_map(i, j):
        return (i // tiles_per_batch, i % tiles_per_batch, 0)

    def mod_map(i, j):
        return (i // tiles_per_batch, 0, 0)

    def pos_map(i, j):
        return (i % pos_tiles, 0)

    def zo_map(i, j):
        return (jnp.where(j < 3, j, j - 1), i, 0)

    return pl.pallas_call(
        _inproj_kernel,
        grid=(n_tiles, N_WBLK),
        in_specs=[
            pl.BlockSpec((nb, t_rows, D_MODEL), x_map),
            pl.BlockSpec((nb, 1, D_MODEL), mod_map),
            pl.BlockSpec((nb, 1, D_MODEL), mod_map),
            pl.BlockSpec((1, D_MODEL), lambda i, j: (0, 0)),
            pl.BlockSpec((tm, LANES), pos_map),
            pl.BlockSpec((tm, LANES), pos_map),
            pl.BlockSpec((D_MODEL, WBLK), lambda i, j: (0, jnp.minimum(j, N_HEAD_BLK - 1))),
            pl.BlockSpec((D_MODEL, WBLK), lambda i, j: (0, jnp.maximum(j - N_HEAD_BLK, 0))),
            pl.BlockSpec((D_MODEL, 2 * KV_DIM), lambda i, j: (0, 0)),
        ],
        out_specs=[
            pl.BlockSpec((None, tm, WBLK), zo_map),
            pl.BlockSpec((tm, 2 * KV_DIM), lambda i, j: (i, 0)),
        ],
        out_shape=[
            jax.ShapeDtypeStruct((N_ZO_BLK, b * s, WBLK), BF16),
            jax.ShapeDtypeStruct((b * s, 2 * KV_DIM), BF16),
        ],
        scratch_shapes=[
            pltpu.VMEM((tm, D_MODEL), BF16),
            pltpu.VMEM((tm, WBLK), F32),
        ],
        compiler_params=pltpu.CompilerParams(
            dimension_semantics=("arbitrary", "arbitrary"), vmem_limit_bytes=VMEM_LIMIT),
        name="in_proj",
    )(x, shift, scale, norm_g, cos_t, sin_t, w_head, w_rest, w_kv)


def _kv_group_masks():
    lane = lax.broadcasted_iota(jnp.int32, (1, 2 * LANES), 1)
    kmask = [((lane % LANES) // HALF) == g for g in range(N_KV)]
    vmask = [(lane // HEAD_DIM) == g for g in range(N_KV)]
    return kmask, vmask


def _attend_chunk(qs, kwin, vwin, sinks_ref, invalid):
    rows = qs.shape[0]
    cq = rows // GROUP
    rid = lax.broadcasted_iota(jnp.int32, (rows, 1), 0)
    out = None
    for g in range(N_KV):
        s = lax.dot_general(qs, kwin[g], (((1,), (1,)), ((), ())), preferred_element_type=F32)
        s = s * (HEAD_DIM ** -0.5)
        if invalid is not None:
            s = jnp.where(invalid, NEG_INF, s)
        sink = jnp.full((rows, 1), sinks_ref[GROUP * g + GROUP - 1], F32)
        for i in range(GROUP - 2, -1, -1):
            sink = jnp.where(rid < (i + 1) * cq, sinks_ref[GROUP * g + i], sink)
        m = jnp.maximum(jnp.max(s, axis=-1, keepdims=True), sink)
        e = jnp.exp(s - m)
        denom = jnp.sum(e, axis=-1, keepdims=True) + jnp.exp(sink - m)
        p = (e / denom).astype(BF16)
        o = jnp.dot(p, vwin[g], preferred_element_type=F32)
        out = o if out is None else out + o
    return out


def _conv_ln_swish(ubuf, wdw_ref, bdw_ref, lng_ref, lnb_ref, y_scr, row0, n_rows):
    first = HIST_U - (CONV_WIDTH - 1)
    for ct in range(N_CT):
        cs = slice(ct * LANES, (ct + 1) * LANES)
        acc = jnp.broadcast_to(bdw_ref[:, cs], (n_rows, LANES))
        for k in range(CONV_WIDTH):
            acc = acc + wdw_ref[k:k + 1, cs] * ubuf[ct, pl.ds(first + k, n_rows), :]
        y_scr[pl.ds(row0, n_rows), cs] = acc
    y = y_scr[pl.ds(row0, n_rows), :]
    mu = jnp.mean(y, axis=-1, keepdims=True)
    yc = y - mu
    var = jnp.mean(yc * yc, axis=-1, keepdims=True)
    yn = yc * lax.rsqrt(var + LN_EPS) * lng_ref[...] + lnb_ref[...]
    return yn * _sigmoid(yn)


def _out_tail(x, gate, ao, co, smg_ref, wa_ref, wb_ref, wo_ref, fg_ref):
    pa = jnp.dot(ao, wa_ref[...], preferred_element_type=F32)
    pb = jnp.dot(co, wb_ref[...], preferred_element_type=F32)
    sa = jnp.concatenate([smg_ref[0], smg_ref[1]], axis=1).astype(F32)
    sb = jnp.concatenate([smg_ref[2], smg_ref[3]], axis=1).astype(F32)
    merged = sa * pa + sb * pb
    o = jnp.dot(merged.astype(BF16), wo_ref[...], preferred_element_type=F32)
    r = x + gate * o
    ms = jnp.mean(r * r, axis=-1, keepdims=True)
    return r * lax.rsqrt(ms + RMS_EPS) * fg_ref[...]


def _mix_prompt_kernel(sinks_ref, x_ref, gate_ref, q_ref, sga_ref, u_ref, sgb_ref, smg_ref,
                       kv_ref, kvh_ref, uh_ref, wdw_ref, bdw_ref, lng_ref, lnb_ref,
                       wa_ref, wb_ref, wo_ref, fg_ref, y_ref,
                       km_scr, vm_scr, ubuf, cy_scr, ao_scr, co_scr):
    t = pl.program_id(1)
    tm = q_ref.shape[0]
    n_chunks = tm // CHUNK
    first_tile = t == 0

    kmask, vmask = _kv_group_masks()
    kh = kvh_ref[:, 0:KV_DIM]
    vh = kvh_ref[:, KV_DIM:]
    vh = jnp.where(first_tile, jnp.zeros_like(vh), vh)
    kc = kv_ref[:, 0:KV_DIM]
    vc = kv_ref[:, KV_DIM:]
    zero = jnp.zeros((), BF16)
    for g in range(N_KV):
        km_scr[g, 0:WINDOW, :] = jnp.where(kmask[g], kh, zero)
        km_scr[g, WINDOW:, :] = jnp.where(kmask[g], kc, zero)
        vm_scr[g, 0:WINDOW, :] = jnp.where(vmask[g], vh, zero)
        vm_scr[g, WINDOW:, :] = jnp.where(vmask[g], vc, zero)

    nk = WINDOW + CHUNK
    key_id = lax.broadcasted_iota(jnp.int32, (1, nk), 1)
    for c in range(n_chunks):
        r0 = c * CHUNK
        qs = jnp.concatenate(
            [q_ref[r0:r0 + CHUNK, i * 2 * LANES:(i + 1) * 2 * LANES] for i in range(GROUP)], axis=0)
        kwin = [km_scr[g, r0:r0 + nk, :] for g in range(N_KV)]
        vwin = [vm_scr[g, r0:r0 + nk, :] for g in range(N_KV)]
        n_bad = WINDOW - r0
        invalid = (first_tile & (key_id < n_bad)) if n_bad > 0 else None
        o = _attend_chunk(qs, kwin, vwin, sinks_ref, invalid)
        for i in range(GROUP):
            cs = slice(i * 2 * LANES, (i + 1) * 2 * LANES)
            ao_scr[r0:r0 + CHUNK, cs] = (
                o[i * CHUNK:(i + 1) * CHUNK, :] * sga_ref[r0:r0 + CHUNK, cs].astype(F32)).astype(BF16)

    uh = uh_ref[...].astype(F32)
    uh = jnp.where(first_tile, jnp.zeros_like(uh), uh)
    for ct in range(N_CT):
        cs = slice(ct * LANES, (ct + 1) * LANES)
        ubuf[ct, 0:HIST_U, :] = uh[:, cs]
        ubuf[ct, HIST_U:, :] = u_ref[:, cs].astype(F32)
    act = _conv_ln_swish(ubuf, wdw_ref, bdw_ref, lng_ref, lnb_ref, cy_scr, 0, tm)
    co_scr[...] = (act * sgb_ref[...].astype(F32)).astype(BF16)

    y_ref[0] = _out_tail(x_ref[0], gate_ref[0], ao_scr[...], co_scr[...], smg_ref,
                         wa_ref, wb_ref, wo_ref, fg_ref)


def _mix_sample_kernel(sinks_ref, x_ref, gate_ref, q_ref, sga_ref, u_ref, sgb_ref, smg_ref,
                       kv_ref, kvh_ref, uh_ref, wdw_ref, bdw_ref, lng_ref, lnb_ref,
                       wa_ref, wb_ref, wo_ref, fg_ref, y_ref,
                       ubuf, cy_scr, ao_scr, co_scr):
    nb, t_rows, _ = x_ref.shape
    kmask, vmask = _kv_group_masks()
    zero = jnp.zeros((), BF16)
    for b in range(nb):
        r0 = b * t_rows
        kall = jnp.concatenate([kvh_ref[b, :, 0:KV_DIM], kv_ref[r0:r0 + t_rows, 0:KV_DIM]], axis=0)
        vall = jnp.concatenate([kvh_ref[b, :, KV_DIM:], kv_ref[r0:r0 + t_rows, KV_DIM:]], axis=0)
        kwin = [jnp.where(kmask[g], kall, zero) for g in range(N_KV)]
        vwin = [jnp.where(vmask[g], vall, zero) for g in range(N_KV)]
        qs = jnp.concatenate(
            [q_ref[r0:r0 + t_rows, i * 2 * LANES:(i + 1) * 2 * LANES] for i in range(GROUP)], axis=0)
        o = _attend_chunk(qs, kwin, vwin, sinks_ref, None)
        for i in range(GROUP):
            cs = slice(i * 2 * LANES, (i + 1) * 2 * LANES)
            ao_scr[r0:r0 + t_rows, cs] = (
                o[i * t_rows:(i + 1) * t_rows, :] * sga_ref[r0:r0 + t_rows, cs].astype(F32)).astype(BF16)

        for ct in range(N_CT):
            cs = slice(ct * LANES, (ct + 1) * LANES)
            ubuf[ct, 0:HIST_U, :] = uh_ref[b, :, cs]
            ubuf[ct, HIST_U:, :] = u_ref[r0:r0 + t_rows, cs].astype(F32)
        act = _conv_ln_swish(ubuf, wdw_ref, bdw_ref, lng_ref, lnb_ref, cy_scr, r0, t_rows)
        co_scr[r0:r0 + t_rows, :] = (act * sgb_ref[r0:r0 + t_rows, :].astype(F32)).astype(BF16)

    x = x_ref[...]
    out = _out_tail(x.reshape(nb * t_rows, D_MODEL),
                    jnp.broadcast_to(gate_ref[...], x.shape).reshape(nb * t_rows, D_MODEL),
                    ao_scr[...], co_scr[...], smg_ref, wa_ref, wb_ref, wo_ref, fg_ref)
    y_ref[...] = out.reshape(x.shape)


def _const_spec(shape):
    nd = len(shape)
    return pl.BlockSpec(shape, lambda *_: (0,) * nd, pipeline_mode=pl.Buffered(1))


def _weight_specs():
    return [
        _const_spec((HIST_U, CONV_DIM)),
        _const_spec((1, CONV_DIM)),
        _const_spec((1, CONV_DIM)),
        _const_spec((1, CONV_DIM)),
        _const_spec((ATTN_DIM, D_MODEL)),
        _const_spec((CONV_DIM, D_MODEL)),
        _const_spec((D_MODEL, D_MODEL)),
        _const_spec((1, D_MODEL)),
    ]


def _mix_prompt_call(sinks, x, gate, zo, kv, weights, *, tm):
    b, s, _ = x.shape
    tpb = s // tm

    def tile(bi, t):
        return bi * tpb + t

    def kvh_map(bi, t):
        return (jnp.maximum(tile(bi, t) * (tm // WINDOW) - 1, 0), 0)

    def uh_map(bi, t):
        return (2, jnp.maximum(tile(bi, t) * (tm // HIST_U) - 1, 0), 0)

    in_specs = [
        pl.BlockSpec(memory_space=pltpu.SMEM),
        pl.BlockSpec((1, tm, D_MODEL), lambda bi, t: (bi, t, 0)),
        pl.BlockSpec((1, 1, D_MODEL), lambda bi, t: (bi, 0, 0)),
        pl.BlockSpec((None, tm, WBLK), lambda bi, t: (0, tile(bi, t), 0)),
        pl.BlockSpec((None, tm, WBLK), lambda bi, t: (1, tile(bi, t), 0)),
        pl.BlockSpec((None, tm, WBLK), lambda bi, t: (2, tile(bi, t), 0)),
        pl.BlockSpec((None, tm, WBLK), lambda bi, t: (3, tile(bi, t), 0)),
        pl.BlockSpec((4, tm, WBLK), lambda bi, t: (1, tile(bi, t), 0)),
        pl.BlockSpec((tm, 2 * KV_DIM), lambda bi, t: (tile(bi, t), 0)),
        pl.BlockSpec((WINDOW, 2 * KV_DIM), kvh_map),
        pl.BlockSpec((None, HIST_U, WBLK), uh_map),
    ] + _weight_specs()
    return pl.pallas_call(
        _mix_prompt_kernel,
        grid=(b, tpb),
        in_specs=in_specs,
        out_specs=pl.BlockSpec((1, tm, D_MODEL), lambda bi, t: (bi, t, 0)),
        out_shape=jax.ShapeDtypeStruct(x.shape, F32),
        scratch_shapes=[
            pltpu.VMEM((N_KV, WINDOW + tm, KV_DIM), BF16),
            pltpu.VMEM((N_KV, WINDOW + tm, KV_DIM), BF16),
            pltpu.VMEM((N_CT, HIST_U + tm, LANES), F32),
            pltpu.VMEM((tm, CONV_DIM), F32),
            pltpu.VMEM((tm, ATTN_DIM), BF16),
            pltpu.VMEM((tm, CONV_DIM), BF16),
        ],
        compiler_params=pltpu.CompilerParams(
            dimension_semantics=("arbitrary", "arbitrary"), vmem_limit_bytes=VMEM_LIMIT),
        name="mix_prompt",
    )(sinks, x, gate, zo, zo, zo, zo, zo, kv, kv, zo, *weights)


def _mix_sample_call(sinks, x, gate, zo, kv, kv_hist, u_hist, weights, *, nb):
    b, t_rows, _ = x.shape
    tm = nb * t_rows
    in_specs = [
        pl.BlockSpec(memory_space=pltpu.SMEM),
        pl.BlockSpec((nb, t_rows, D_MODEL), lambda i: (i, 0, 0)),
        pl.BlockSpec((nb, 1, D_MODEL), lambda i: (i, 0, 0)),
        pl.BlockSpec((None, tm, WBLK), lambda i: (0, i, 0)),
        pl.BlockSpec((None, tm, WBLK), lambda i: (1, i, 0)),
        pl.BlockSpec((None, tm, WBLK), lambda i: (2, i, 0)),
        pl.BlockSpec((None, tm, WBLK), lambda i: (3, i, 0)),
        pl.BlockSpec((4, tm, WBLK), lambda i: (1, i, 0)),
        pl.BlockSpec((tm, 2 * KV_DIM), lambda i: (i, 0)),
        pl.BlockSpec((nb, WINDOW, 2 * KV_DIM), lambda i: (i, 0, 0)),
        pl.BlockSpec((nb, HIST_U, CONV_DIM), lambda i: (i, 0, 0)),
    ] + _weight_specs()
    return pl.pallas_call(
        _mix_sample_kernel,
        grid=(b // nb,),
        in_specs=in_specs,
        out_specs=pl.BlockSpec((nb, t_rows, D_MODEL), lambda i: (i, 0, 0)),
        out_shape=jax.ShapeDtypeStruct(x.shape, F32),
        scratch_shapes=[
            pltpu.VMEM((N_CT, HIST_U + t_rows, LANES), F32),
            pltpu.VMEM((tm, CONV_DIM), F32),
            pltpu.VMEM((tm, ATTN_DIM), BF16),
            pltpu.VMEM((tm, CONV_DIM), BF16),
        ],
        compiler_params=pltpu.CompilerParams(
            dimension_semantics=("arbitrary",), vmem_limit_bytes=VMEM_LIMIT),
        name="mix_sample",
    )(sinks, x, gate, zo, zo, zo, zo, zo, kv, kv_hist, u_hist, *weights)


def _q_cols(w):
    lead = w.shape[:-1]
    w = w.reshape(*lead, N_KV, GROUP, 2, HALF)
    return jnp.moveaxis(w, (-4, -3, -2), (-2, -4, -3)).reshape(*lead, ATTN_DIM)


def _k_cols(w):
    lead = w.shape[:-1]
    w = w.reshape(*lead, N_KV, 2, HALF)
    return jnp.swapaxes(w, -3, -2).reshape(*lead, KV_DIM)


def _k_cols_inv(w):
    lead = w.shape[:-1]
    w = w.reshape(*lead, 2, N_KV, HALF)
    return jnp.swapaxes(w, -3, -2).reshape(*lead, KV_DIM)


def _a_cols(w):
    lead = w.shape[:-1]
    w = w.reshape(*lead, N_KV, GROUP, HEAD_DIM)
    return jnp.swapaxes(w, -3, -2).reshape(*lead, ATTN_DIM)


def _rope_tables(pos):
    inv = ROPE_THETA ** (-2.0 * jnp.arange(HALF, dtype=F32) / HEAD_DIM)
    ang = pos.astype(F32)[:, None] * inv[None, :]
    return jnp.tile(jnp.cos(ang), (1, N_KV)), jnp.tile(jnp.sin(ang), (1, N_KV))


def _layer(l, xp, xs, c_prompt, c_sample, cache_k, cache_v, state_conv, norm_g, w_ada, b_ada, w_in,
           sinks, w_dw, b_dw, ln_g, ln_b, w_proj_a, w_proj_b, w_out, final_g):
    b, s, _ = xp.shape
    db, ds, _ = xs.shape
    n_win = cache_k.shape[2]
    in_tm = min(s, 1024)
    assert n_win == WINDOW and s % in_tm == 0 and s % 256 == 0 and db % 8 == 0 and ds % 16 == 0

    wi = w_in[l]
    w_head = jnp.concatenate([_q_cols(wi[:, :Q_END]), _a_cols(wi[:, V_END:GA_END])], axis=1).astype(BF16)
    w_rest = wi[:, GA_END:].astype(BF16)
    w_kv = jnp.concatenate([_k_cols(wi[:, Q_END:K_END]), wi[:, K_END:V_END]], axis=1).astype(BF16)
    wa = jnp.swapaxes(w_proj_a[l].reshape(N_KV, GROUP, HEAD_DIM, D_MODEL), 0, 1)
    wa = wa.reshape(ATTN_DIM, D_MODEL).astype(BF16)
    wb = w_proj_b[l].astype(BF16)
    wo = w_out[l].astype(BF16)
    wdw = jnp.pad(w_dw[l], ((0, HIST_U - CONV_WIDTH), (0, 0)))
    weights = (wdw, b_dw[l][None], ln_g[l][None], ln_b[l][None], wa, wb, wo, final_g[None, :])

    pad_rows = (-(b + db)) % 16
    c_all = jnp.concatenate([c_prompt, c_sample, jnp.zeros((pad_rows, D_MODEL), F32)], axis=0)
    mod = _mod_call(c_all, w_ada[l], b_ada[l][None])
    shift, scale, gate = (mod[:, i * D_MODEL:(i + 1) * D_MODEL][:, None, :] for i in range(3))

    cos_p, sin_p = _rope_tables(jnp.arange(s))
    cos_s, sin_s = _rope_tables(PAST_LEN + jnp.arange(ds))
    cos_s, sin_s = jnp.tile(cos_s, (db, 1)), jnp.tile(sin_s, (db, 1))

    zo_p, kv_p = _inproj_call(xp, shift[:b], scale[:b], norm_g[l][None], cos_p, sin_p,
                              w_head, w_rest, w_kv, nb=1, t_rows=in_tm)
    yp = _mix_prompt_call(sinks[l], xp, gate[:b], zo_p, kv_p, weights, tm=256)
    kv3 = kv_p.reshape(b, s, 2 * KV_DIM)[:, s - n_win:].astype(F32)
    new_k_p = _k_cols_inv(kv3[..., :KV_DIM]).reshape(b, n_win, N_KV, HEAD_DIM)
    new_v_p = kv3[..., KV_DIM:].reshape(b, n_win, N_KV, HEAD_DIM)
    new_c_p = zo_p.reshape(N_ZO_BLK, b, s, CONV_DIM)[2, :, s - (CONV_WIDTH - 1):].astype(F32)

    zo_s, kv_s = _inproj_call(xs, shift[b:b + db], scale[b:b + db], norm_g[l][None], cos_s, sin_s,
                              w_head, w_rest, w_kv, nb=db, t_rows=ds)
    kv_hist = jnp.concatenate(
        [_k_cols(cache_k[l].reshape(db, n_win, KV_DIM)), cache_v[l].reshape(db, n_win, KV_DIM)],
        axis=-1).astype(BF16)
    u_hist = jnp.pad(state_conv[l], ((0, 0), (HIST_U - (CONV_WIDTH - 1), 0), (0, 0)))
    ys = _mix_sample_call(sinks[l], xs, gate[b:b + db], zo_s, kv_s, kv_hist, u_hist, weights, nb=8)
    kvs3 = kv_s.reshape(db, ds, 2 * KV_DIM).astype(F32)
    k_new = _k_cols_inv(kvs3[..., :KV_DIM]).reshape(db, ds, N_KV, HEAD_DIM)
    v_new = kvs3[..., KV_DIM:].reshape(db, ds, N_KV, HEAD_DIM)
    new_k_s = jnp.concatenate([cache_k[l], k_new], axis=1)[:, -n_win:]
    new_v_s = jnp.concatenate([cache_v[l], v_new], axis=1)[:, -n_win:]
    u_new = zo_s.reshape(N_ZO_BLK, db, ds, CONV_DIM)[2].astype(F32)
    new_c_s = jnp.concatenate([state_conv[l], u_new], axis=1)[:, -(CONV_WIDTH - 1):]
    return yp, ys, new_k_p, new_v_p, new_c_p, new_k_s, new_v_s, new_c_s


def kernel(x_prompt, x_sample, c_prompt, c_sample, cache_k, cache_v, state_conv, norm_g, w_ada, b_ada,
           w_in, sinks, w_dw, b_dw, ln_g, ln_b, w_proj_a, w_proj_b, w_out, final_g):
    assert w_in.shape[0] == 1
    res = _layer(0, x_prompt, x_sample, c_prompt, c_sample, cache_k, cache_v, state_conv, norm_g, w_ada,
                 b_ada, w_in, sinks, w_dw, b_dw, ln_g, ln_b, w_proj_a, w_proj_b, w_out, final_g)
    return res[:2] + tuple(r[None] for r in res[2:])
```

```python
import jax
import jax.numpy as jnp
from jax import lax
from jax.experimental import pallas as pl
from jax.experimental.pallas import tpu as pltpu

D_MODEL = 2048
N_HEADS = 16
N_KV = 4
HEAD_DIM = 64
HALF = HEAD_DIM // 2
GROUP = N_HEADS // N_KV
ATTN_DIM = N_HEADS * HEAD_DIM
KV_DIM = N_KV * HEAD_DIM
CONV_DIM = 1024
CONV_WIDTH = 31
CHUNK = 64
WINDOW = 128
ROPE_THETA = 10000.0
RMS_EPS = 1e-6
LN_EPS = 1e-5
NEG_INF = -1e30
LOG2E = 1.4426950408889634
PAST_LEN = 1024

Q_END = ATTN_DIM
K_END = Q_END + KV_DIM
V_END = K_END + KV_DIM
GA_END = V_END + ATTN_DIM
CU_END = GA_END + 2 * CONV_DIM
GB_END = CU_END + CONV_DIM
IN_DIM = GB_END + 2 * D_MODEL

LANES = 128
HIST_U = 32
N_CT = CONV_DIM // LANES
VMEM_LIMIT = 56 * 1024 * 1024

N_ZO_BLK = 8
WBLK = 1024
N_HEAD_BLK = 2
N_WBLK = 9
ROW_SUB = 256
NORM_ROWS = 128

BF16 = jnp.bfloat16
F32 = jnp.float32


def _sigmoid(x):
    return 1.0 / (1.0 + jnp.exp(-x))


def _mod_kernel(c_ref, w_ref, b_ref, o_ref):
    c = c_ref[...]
    a = (c * _sigmoid(c)).astype(BF16)
    w = w_ref[...].astype(BF16)
    o_ref[...] = jnp.dot(a, w, preferred_element_type=F32) + b_ref[...]


def _mod_call(c_all, w_ada, b_ada):
    rows = c_all.shape[0]
    n = w_ada.shape[1]
    tn = 1024
    return pl.pallas_call(
        _mod_kernel,
        grid=(n // tn,),
        in_specs=[
            pl.BlockSpec((rows, D_MODEL), lambda j: (0, 0)),
            pl.BlockSpec((D_MODEL, tn), lambda j: (0, j)),
            pl.BlockSpec((1, tn), lambda j: (0, j)),
        ],
        out_specs=pl.BlockSpec((rows, tn), lambda j: (0, j)),
        out_shape=jax.ShapeDtypeStruct((rows, n), F32),
        compiler_params=pltpu.CompilerParams(
            dimension_semantics=("arbitrary",), vmem_limit_bytes=VMEM_LIMIT),
        name="adaln_mod",
    )(c_all, w_ada, b_ada)


def _rope_pair(a, b, cos, sin):
    return a * cos - b * sin, b * cos + a * sin


def _inproj_kernel(x_ref, shift_ref, scale_ref, g_ref, cos_ref, sin_ref, wh_ref, wl_ref, wu_ref, wkv_ref,
                   zo_ref, kv_ref, h_scr, glu_scr):
    j = pl.program_id(1)
    nb, t_rows, _ = x_ref.shape
    tm = nb * t_rows
    sub = min(tm, ROW_SUB)
    row_blocks = range(0, tm, sub)
    wr_ref = (wl_ref, wu_ref)

    def proj(w_ref, r0):
        h = h_scr[r0:r0 + sub, :]
        if isinstance(w_ref, tuple):
            return jnp.concatenate(
                [jnp.dot(h, w[...], preferred_element_type=F32) for w in w_ref], axis=1)
        return jnp.dot(h, w_ref[...], preferred_element_type=F32)

    def norm_rows(r0, rc):
        if rc <= t_rows:
            bi, t0 = divmod(r0, t_rows)
            x = x_ref[bi, t0:t0 + rc, :]
            scale, shift = scale_ref[bi], shift_ref[bi]
        else:
            b0, nbc = r0 // t_rows, rc // t_rows
            x = x_ref[b0:b0 + nbc]
            scale, shift = scale_ref[b0:b0 + nbc], shift_ref[b0:b0 + nbc]
        ms = jnp.mean(x * x, axis=-1, keepdims=True)
        y = x * lax.rsqrt(ms + RMS_EPS) * g_ref[...]
        h = y * (1.0 + scale) + shift
        h_scr[r0:r0 + rc, :] = h.reshape(rc, D_MODEL).astype(BF16)

    @pl.when(j == 0)
    def _():
        rc = min(sub, NORM_ROWS)
        for r0 in row_blocks:
            for c0 in range(r0, r0 + sub, rc):
                norm_rows(c0, rc)
            rows = slice(r0, r0 + sub)
            cos = cos_ref[rows, :]
            sin = sin_ref[rows, :]
            acc = jnp.dot(h_scr[rows, :], wkv_ref[...], preferred_element_type=F32)
            ka, kb = _rope_pair(acc[:, 0:LANES], acc[:, LANES:2 * LANES], cos, sin)
            kv_ref[rows, 0:LANES] = ka.astype(BF16)
            kv_ref[rows, LANES:2 * LANES] = kb.astype(BF16)
            kv_ref[rows, 2 * LANES:] = acc[:, 2 * LANES:].astype(BF16)
            acc = proj(wh_ref, r0)
            for i in range(GROUP):
                lo = i * 2 * LANES
                qa, qb = _rope_pair(acc[:, lo:lo + LANES], acc[:, lo + LANES:lo + 2 * LANES], cos, sin)
                zo_ref[rows, lo:lo + LANES] = qa.astype(BF16)
                zo_ref[rows, lo + LANES:lo + 2 * LANES] = qb.astype(BF16)

    @pl.when(j == 1)
    def _():
        for r0 in row_blocks:
            acc = proj(wh_ref, r0)
            zo_ref[r0:r0 + sub, :] = (acc * _sigmoid(acc)).astype(BF16)

    @pl.when(j == 2)
    def _():
        for r0 in row_blocks:
            glu_scr[r0:r0 + sub, :] = proj(wr_ref, r0)

    @pl.when(j == 3)
    def _():
        for r0 in row_blocks:
            zo_ref[r0:r0 + sub, :] = (glu_scr[r0:r0 + sub, :] * _sigmoid(proj(wr_ref, r0))).astype(BF16)

    @pl.when(j == 4)
    def _():
        for r0 in row_blocks:
            acc = proj(wr_ref, r0)
            zo_ref[r0:r0 + sub, :] = (acc * _sigmoid(acc)).astype(BF16)

    @pl.when(j >= 5)
    def _():
        for r0 in row_blocks:
            zo_ref[r0:r0 + sub, :] = _sigmoid(proj(wr_ref, r0)).astype(BF16)


def _inproj_call(x, shift, scale, norm_g, cos_t, sin_t, w_head, w_all, w_kv, *, nb, t_rows):
    half = WBLK // 2
    first_half = GA_END // half
    b, s, _ = x.shape
    tm = nb * t_rows
    tiles_per_batch = s // t_rows
    n_tiles = (b // nb) * tiles_per_batch
    pos_tiles = cos_t.shape[0] // tm

    def x_map(i, j):
        return (i // tiles_per_batch, i % tiles_per_batch, 0)

    def mod_map(i, j):
        return (i // tiles_per_batch, 0, 0)

    def pos_map(i, j):
        return (i % pos_tiles, 0)

    def zo_map(i, j):
        return (jnp.where(j < 3, j, j - 1), i, 0)

    return pl.pallas_call(
        _inproj_kernel,
        grid=(n_tiles, N_WBLK),
        in_specs=[
            pl.BlockSpec((nb, t_rows, D_MODEL), x_map),
            pl.BlockSpec((nb, 1, D_MODEL), mod_map),
            pl.BlockSpec((nb, 1, D_MODEL), mod_map),
            pl.BlockSpec((1, D_MODEL), lambda i, j: (0, 0)),
            pl.BlockSpec((tm, LANES), pos_map),
            pl.BlockSpec((tm, LANES), pos_map),
            pl.BlockSpec((D_MODEL, WBLK), lambda i, j: (0, jnp.minimum(j, N_HEAD_BLK - 1))),
            pl.BlockSpec((D_MODEL, half),
                         lambda i, j: (0, first_half + 2 * jnp.maximum(j - N_HEAD_BLK, 0))),
            pl.BlockSpec((D_MODEL, half),
                         lambda i, j: (0, first_half + 2 * jnp.maximum(j - N_HEAD_BLK, 0) + 1)),
            pl.BlockSpec((D_MODEL, 2 * KV_DIM), lambda i, j: (0, 0)),
        ],
        out_specs=[
            pl.BlockSpec((None, tm, WBLK), zo_map),
            pl.BlockSpec((tm, 2 * KV_DIM), lambda i, j: (i, 0)),
        ],
        out_shape=[
            jax.ShapeDtypeStruct((N_ZO_BLK, b * s, WBLK), BF16),
            jax.ShapeDtypeStruct((b * s, 2 * KV_DIM), BF16),
        ],
        scratch_shapes=[
            pltpu.VMEM((tm, D_MODEL), BF16),
            pltpu.VMEM((tm, WBLK), F32),
        ],
        compiler_params=pltpu.CompilerParams(
            dimension_semantics=("arbitrary", "arbitrary"), vmem_limit_bytes=VMEM_LIMIT),
        name="in_proj",
    )(x, shift, scale, norm_g, cos_t, sin_t, w_head, w_all, w_all, w_kv)


def _kv_group_masks():
    lane = lax.broadcasted_iota(jnp.int32, (1, 2 * LANES), 1)
    kmask = [((lane % LANES) // HALF) == g for g in range(N_KV)]
    vmask = [(lane // HEAD_DIM) == g for g in range(N_KV)]
    return kmask, vmask


def _sink_columns(sinks_ref, rows):
    cq = rows // GROUP
    rid = lax.broadcasted_iota(jnp.int32, (rows, 1), 0)
    cols = []
    for g in range(N_KV):
        sink = jnp.full((rows, 1), sinks_ref[GROUP * g + GROUP - 1], F32)
        for i in range(GROUP - 2, -1, -1):
            sink = jnp.where(rid < (i + 1) * cq, sinks_ref[GROUP * g + i], sink)
        cols.append(sink * LOG2E)
    return cols


def _attend_chunk(qs, kwin, vwin, sink_cols, invalid):
    out = None
    for g in range(N_KV):
        s = lax.dot_general(qs, kwin[g], (((1,), (1,)), ((), ())), preferred_element_type=F32)
        s = s * (HEAD_DIM ** -0.5 * LOG2E)
        if invalid is not None:
            s = jnp.where(invalid, NEG_INF, s)
        sink = sink_cols[g]
        m = jnp.maximum(jnp.max(s, axis=-1, keepdims=True), sink)
        e = jnp.exp2(s - m)
        denom = jnp.sum(e, axis=-1, keepdims=True) + jnp.exp2(sink - m)
        p = (e / denom).astype(BF16)
        o = jnp.dot(p, vwin[g], preferred_element_type=F32)
        out = o if out is None else out + o
    return out


def _conv_ln_swish(ubuf, wdw_ref, bdw_ref, lng_ref, lnb_ref, y_scr, row0, n_rows):
    first = HIST_U - (CONV_WIDTH - 1)
    for ct in range(N_CT):
        cs = slice(ct * LANES, (ct + 1) * LANES)
        acc = jnp.broadcast_to(bdw_ref[:, cs], (n_rows, LANES))
        for k in range(CONV_WIDTH):
            acc = acc + wdw_ref[k:k + 1, cs] * ubuf[ct, pl.ds(first + k, n_rows), :]
        y_scr[pl.ds(row0, n_rows), cs] = acc
    y = y_scr[pl.ds(row0, n_rows), :]
    mu = jnp.mean(y, axis=-1, keepdims=True)
    yc = y - mu
    var = jnp.mean(yc * yc, axis=-1, keepdims=True)
    yn = yc * lax.rsqrt(var + LN_EPS) * lng_ref[...] + lnb_ref[...]
    return yn * _sigmoid(yn)


def _out_tail(x, gate, ao, co, smg_ref, wa_ref, wb_ref, wo_ref, fg_ref):
    pa = jnp.dot(ao, wa_ref[...], preferred_element_type=F32)
    pb = jnp.dot(co, wb_ref[...], preferred_element_type=F32)
    sa = jnp.concatenate([smg_ref[0], smg_ref[1]], axis=1).astype(F32)
    sb = jnp.concatenate([smg_ref[2], smg_ref[3]], axis=1).astype(F32)
    merged = sa * pa + sb * pb
    o = jnp.dot(merged.astype(BF16), wo_ref[...], preferred_element_type=F32)
    r = x + gate * o
    ms = jnp.mean(r * r, axis=-1, keepdims=True)
    return r * lax.rsqrt(ms + RMS_EPS) * fg_ref[...]


def _mix_prompt_kernel(sinks_ref, x_ref, gate_ref, q_ref, sga_ref, u_ref, sgb_ref, smg_ref,
                       kv_ref, kvh_ref, uh_ref, wdw_ref, bdw_ref, lng_ref, lnb_ref,
                       wa_ref, wb_ref, wo_ref, fg_ref, y_ref,
                       km_scr, vm_scr, ubuf, cy_scr, ao_scr, co_scr):
    t = pl.program_id(1)
    tm = q_ref.shape[0]
    n_chunks = tm // CHUNK
    first_tile = t == 0

    kmask, vmask = _kv_group_masks()
    kh = kvh_ref[:, 0:KV_DIM]
    vh = kvh_ref[:, KV_DIM:]
    vh = jnp.where(first_tile, jnp.zeros_like(vh), vh)
    kc = kv_ref[:, 0:KV_DIM]
    vc = kv_ref[:, KV_DIM:]
    zero = jnp.zeros((), BF16)
    for g in range(N_KV):
        km_scr[g, 0:WINDOW, :] = jnp.where(kmask[g], kh, zero)
        km_scr[g, WINDOW:, :] = jnp.where(kmask[g], kc, zero)
        vm_scr[g, 0:WINDOW, :] = jnp.where(vmask[g], vh, zero)
        vm_scr[g, WINDOW:, :] = jnp.where(vmask[g], vc, zero)

    nk = WINDOW + CHUNK
    key_id = lax.broadcasted_iota(jnp.int32, (1, nk), 1)
    sink_cols = _sink_columns(sinks_ref, GROUP * CHUNK)
    for c in range(n_chunks):
        r0 = c * CHUNK
        qs = jnp.concatenate(
            [q_ref[r0:r0 + CHUNK, i * 2 * LANES:(i + 1) * 2 * LANES] for i in range(GROUP)], axis=0)
        kwin = [km_scr[g, r0:r0 + nk, :] for g in range(N_KV)]
        vwin = [vm_scr[g, r0:r0 + nk, :] for g in range(N_KV)]
        n_bad = WINDOW - r0
        invalid = (first_tile & (key_id < n_bad)) if n_bad > 0 else None
        o = _attend_chunk(qs, kwin, vwin, sink_cols, invalid)
        for i in range(GROUP):
            cs = slice(i * 2 * LANES, (i + 1) * 2 * LANES)
            ao_scr[r0:r0 + CHUNK, cs] = (
                o[i * CHUNK:(i + 1) * CHUNK, :] * sga_ref[r0:r0 + CHUNK, cs].astype(F32)).astype(BF16)

    uh = uh_ref[...].astype(F32)
    uh = jnp.where(first_tile, jnp.zeros_like(uh), uh)
    for ct in range(N_CT):
        cs = slice(ct * LANES, (ct + 1) * LANES)
        ubuf[ct, 0:HIST_U, :] = uh[:, cs]
        ubuf[ct, HIST_U:, :] = u_ref[:, cs].astype(F32)
    act = _conv_ln_swish(ubuf, wdw_ref, bdw_ref, lng_ref, lnb_ref, cy_scr, 0, tm)
    co_scr[...] = (act * sgb_ref[...].astype(F32)).astype(BF16)

    y_ref[0] = _out_tail(x_ref[0], gate_ref[0], ao_scr[...], co_scr[...], smg_ref,
                         wa_ref, wb_ref, wo_ref, fg_ref)


def _mix_sample_kernel(sinks_ref, x_ref, gate_ref, q_ref, sga_ref, u_ref, sgb_ref, smg_ref,
                       kv_ref, kvh_ref, uh_ref, wdw_ref, bdw_ref, lng_ref, lnb_ref,
                       wa_ref, wb_ref, wo_ref, fg_ref, y_ref,
                       ubuf, cy_scr, ao_scr, co_scr):
    nb, t_rows, _ = x_ref.shape
    kmask, vmask = _kv_group_masks()
    zero = jnp.zeros((), BF16)
    sink_cols = _sink_columns(sinks_ref, GROUP * t_rows)
    for b in range(nb):
        r0 = b * t_rows
        kall = jnp.concatenate([kvh_ref[b, :, 0:KV_DIM], kv_ref[r0:r0 + t_rows, 0:KV_DIM]], axis=0)
        vall = jnp.concatenate([kvh_ref[b, :, KV_DIM:], kv_ref[r0:r0 + t_rows, KV_DIM:]], axis=0)
        kwin = [jnp.where(kmask[g], kall, zero) for g in range(N_KV)]
        vwin = [jnp.where(vmask[g], vall, zero) for g in range(N_KV)]
        qs = jnp.concatenate(
            [q_ref[r0:r0 + t_rows, i * 2 * LANES:(i + 1) * 2 * LANES] for i in range(GROUP)], axis=0)
        o = _attend_chunk(qs, kwin, vwin, sink_cols, None)
        for i in range(GROUP):
            cs = slice(i * 2 * LANES, (i + 1) * 2 * LANES)
            ao_scr[r0:r0 + t_rows, cs] = (
                o[i * t_rows:(i + 1) * t_rows, :] * sga_ref[r0:r0 + t_rows, cs].astype(F32)).astype(BF16)

        for ct in range(N_CT):
            cs = slice(ct * LANES, (ct + 1) * LANES)
            ubuf[ct, 0:HIST_U, :] = uh_ref[b, :, cs]
            ubuf[ct, HIST_U:, :] = u_ref[r0:r0 + t_rows, cs].astype(F32)
        act = _conv_ln_swish(ubuf, wdw_ref, bdw_ref, lng_ref, lnb_ref, cy_scr, r0, t_rows)
        co_scr[r0:r0 + t_rows, :] = (act * sgb_ref[r0:r0 + t_rows, :].astype(F32)).astype(BF16)

    x = x_ref[...]
    out = _out_tail(x.reshape(nb * t_rows, D_MODEL),
                    jnp.broadcast_to(gate_ref[...], x.shape).reshape(nb * t_rows, D_MODEL),
                    ao_scr[...], co_scr[...], smg_ref, wa_ref, wb_ref, wo_ref, fg_ref)
    y_ref[...] = out.reshape(x.shape)


def _const_spec(shape):
    nd = len(shape)
    return pl.BlockSpec(shape, lambda *_: (0,) * nd, pipeline_mode=pl.Buffered(1))


def _weight_specs():
    return [
        _const_spec((HIST_U, CONV_DIM)),
        _const_spec((1, CONV_DIM)),
        _const_spec((1, CONV_DIM)),
        _const_spec((1, CONV_DIM)),
        _const_spec((ATTN_DIM, D_MODEL)),
        _const_spec((CONV_DIM, D_MODEL)),
        _const_spec((D_MODEL, D_MODEL)),
        _const_spec((1, D_MODEL)),
    ]


def _mix_prompt_call(sinks, x, gate, zo, kv, weights, *, tm):
    b, s, _ = x.shape
    tpb = s // tm

    def tile(bi, t):
        return bi * tpb + t

    def kvh_map(bi, t):
        return (jnp.maximum(tile(bi, t) * (tm // WINDOW) - 1, 0), 0)

    def uh_map(bi, t):
        return (2, jnp.maximum(tile(bi, t) * (tm // HIST_U) - 1, 0), 0)

    in_specs = [
        pl.BlockSpec(memory_space=pltpu.SMEM),
        pl.BlockSpec((1, tm, D_MODEL), lambda bi, t: (bi, t, 0)),
        pl.BlockSpec((1, 1, D_MODEL), lambda bi, t: (bi, 0, 0)),
        pl.BlockSpec((None, tm, WBLK), lambda bi, t: (0, tile(bi, t), 0)),
        pl.BlockSpec((None, tm, WBLK), lambda bi, t: (1, tile(bi, t), 0)),
        pl.BlockSpec((None, tm, WBLK), lambda bi, t: (2, tile(bi, t), 0)),
        pl.BlockSpec((None, tm, WBLK), lambda bi, t: (3, tile(bi, t), 0)),
        pl.BlockSpec((4, tm, WBLK), lambda bi, t: (1, tile(bi, t), 0)),
        pl.BlockSpec((tm, 2 * KV_DIM), lambda bi, t: (tile(bi, t), 0)),
        pl.BlockSpec((WINDOW, 2 * KV_DIM), kvh_map),
        pl.BlockSpec((None, HIST_U, WBLK), uh_map),
    ] + _weight_specs()
    return pl.pallas_call(
        _mix_prompt_kernel,
        grid=(b, tpb),
        in_specs=in_specs,
        out_specs=pl.BlockSpec((1, tm, D_MODEL), lambda bi, t: (bi, t, 0)),
        out_shape=jax.ShapeDtypeStruct(x.shape, F32),
        scratch_shapes=[
            pltpu.VMEM((N_KV, WINDOW + tm, KV_DIM), BF16),
            pltpu.VMEM((N_KV, WINDOW + tm, KV_DIM), BF16),
            pltpu.VMEM((N_CT, HIST_U + tm, LANES), F32),
            pltpu.VMEM((tm, CONV_DIM), F32),
            pltpu.VMEM((tm, ATTN_DIM), BF16),
            pltpu.VMEM((tm, CONV_DIM), BF16),
        ],
        compiler_params=pltpu.CompilerParams(
            dimension_semantics=("arbitrary", "arbitrary"), vmem_limit_bytes=VMEM_LIMIT),
        name="mix_prompt",
    )(sinks, x, gate, zo, zo, zo, zo, zo, kv, kv, zo, *weights)


def _mix_sample_call(sinks, x, gate, zo, kv, kv_hist, u_hist, weights, *, nb):
    b, t_rows, _ = x.shape
    tm = nb * t_rows
    in_specs = [
        pl.BlockSpec(memory_space=pltpu.SMEM),
        pl.BlockSpec((nb, t_rows, D_MODEL), lambda i: (i, 0, 0)),
        pl.BlockSpec((nb, 1, D_MODEL), lambda i: (i, 0, 0)),
        pl.BlockSpec((None, tm, WBLK), lambda i: (0, i, 0)),
        pl.BlockSpec((None, tm, WBLK), lambda i: (1, i, 0)),
        pl.BlockSpec((None, tm, WBLK), lambda i: (2, i, 0)),
        pl.BlockSpec((None, tm, WBLK), lambda i: (3, i, 0)),
        pl.BlockSpec((4, tm, WBLK), lambda i: (1, i, 0)),
        pl.BlockSpec((tm, 2 * KV_DIM), lambda i: (i, 0)),
        pl.BlockSpec((nb, WINDOW, 2 * KV_DIM), lambda i: (i, 0, 0)),
        pl.BlockSpec((nb, HIST_U, CONV_DIM), lambda i: (i, 0, 0)),
    ] + _weight_specs()
    return pl.pallas_call(
        _mix_sample_kernel,
        grid=(b // nb,),
        in_specs=in_specs,
        out_specs=pl.BlockSpec((nb, t_rows, D_MODEL), lambda i: (i, 0, 0)),
        out_shape=jax.ShapeDtypeStruct(x.shape, F32),
        scratch_shapes=[
            pltpu.VMEM((N_CT, HIST_U + t_rows, LANES), F32),
            pltpu.VMEM((tm, CONV_DIM), F32),
            pltpu.VMEM((tm, ATTN_DIM), BF16),
            pltpu.VMEM((tm, CONV_DIM), BF16),
        ],
        compiler_params=pltpu.CompilerParams(
            dimension_semantics=("arbitrary",), vmem_limit_bytes=VMEM_LIMIT),
        name="mix_sample",
    )(sinks, x, gate, zo, zo, zo, zo, zo, kv, kv_hist, u_hist, *weights)


def _q_cols(w):
    lead = w.shape[:-1]
    w = w.reshape(*lead, N_KV, GROUP, 2, HALF)
    return jnp.moveaxis(w, (-4, -3, -2), (-2, -4, -3)).reshape(*lead, ATTN_DIM)


def _k_cols(w):
    lead = w.shape[:-1]
    w = w.reshape(*lead, N_KV, 2, HALF)
    return jnp.swapaxes(w, -3, -2).reshape(*lead, KV_DIM)


def _k_cols_inv(w):
    lead = w.shape[:-1]
    w = w.reshape(*lead, 2, N_KV, HALF)
    return jnp.swapaxes(w, -3, -2).reshape(*lead, KV_DIM)


def _a_cols(w):
    lead = w.shape[:-1]
    w = w.reshape(*lead, N_KV, GROUP, HEAD_DIM)
    return jnp.swapaxes(w, -3, -2).reshape(*lead, ATTN_DIM)


def _rope_tables(pos):
    inv = ROPE_THETA ** (-2.0 * jnp.arange(HALF, dtype=F32) / HEAD_DIM)
    ang = pos.astype(F32)[:, None] * inv[None, :]
    return jnp.tile(jnp.cos(ang), (1, N_KV)), jnp.tile(jnp.sin(ang), (1, N_KV))


def _layer(l, xp, xs, c_prompt, c_sample, cache_k, cache_v, state_conv, norm_g, w_ada, b_ada, w_in,
           sinks, w_dw, b_dw, ln_g, ln_b, w_proj_a, w_proj_b, w_out, final_g):
    b, s, _ = xp.shape
    db, ds, _ = xs.shape
    n_win = cache_k.shape[2]
    in_tm = min(s, 1024)
    assert n_win == WINDOW and s % in_tm == 0 and s % 256 == 0 and db % 8 == 0 and ds % 16 == 0

    wi = w_in[l].astype(BF16)
    w_head = jnp.concatenate([_q_cols(wi[:, :Q_END]), _a_cols(wi[:, V_END:GA_END])], axis=1)
    w_kv = jnp.concatenate([_k_cols(wi[:, Q_END:K_END]), wi[:, K_END:V_END]], axis=1)
    wa = jnp.swapaxes(w_proj_a[l].reshape(N_KV, GROUP, HEAD_DIM, D_MODEL), 0, 1)
    wa = wa.reshape(ATTN_DIM, D_MODEL).astype(BF16)
    wb = w_proj_b[l].astype(BF16)
    wo = w_out[l].astype(BF16)
    wdw = jnp.pad(w_dw[l], ((0, HIST_U - CONV_WIDTH), (0, 0)))
    weights = (wdw, b_dw[l][None], ln_g[l][None], ln_b[l][None], wa, wb, wo, final_g[None, :])

    pad_rows = (-(b + db)) % 16
    c_all = jnp.concatenate([c_prompt, c_sample, jnp.zeros((pad_rows, D_MODEL), F32)], axis=0)
    mod = _mod_call(c_all, w_ada[l], b_ada[l][None])
    shift, scale, gate = (mod[:, i * D_MODEL:(i + 1) * D_MODEL][:, None, :] for i in range(3))

    cos_p, sin_p = _rope_tables(jnp.arange(s))
    cos_s, sin_s = _rope_tables(PAST_LEN + jnp.arange(ds))
    cos_s, sin_s = jnp.tile(cos_s, (db, 1)), jnp.tile(sin_s, (db, 1))

    zo_p, kv_p = _inproj_call(xp, shift[:b], scale[:b], norm_g[l][None], cos_p, sin_p,
                              w_head, wi, w_kv, nb=1, t_rows=in_tm)
    yp = _mix_prompt_call(sinks[l], xp, gate[:b], zo_p, kv_p, weights, tm=256)
    kv3 = kv_p.reshape(b, s, 2 * KV_DIM)[:, s - n_win:].astype(F32)
    new_k_p = _k_cols_inv(kv3[..., :KV_DIM]).reshape(b, n_win, N_KV, HEAD_DIM)
    new_v_p = kv3[..., KV_DIM:].reshape(b, n_win, N_KV, HEAD_DIM)
    new_c_p = zo_p.reshape(N_ZO_BLK, b, s, CONV_DIM)[2, :, s - (CONV_WIDTH - 1):].astype(F32)

    zo_s, kv_s = _inproj_call(xs, shift[b:b + db], scale[b:b + db], norm_g[l][None], cos_s, sin_s,
                              w_head, wi, w_kv, nb=db, t_rows=ds)
    kv_hist = jnp.concatenate(
        [_k_cols(cache_k[l].reshape(db, n_win, KV_DIM)), cache_v[l].reshape(db, n_win, KV_DIM)],
        axis=-1).astype(BF16)
    u_hist = jnp.pad(state_conv[l], ((0, 0), (HIST_U - (CONV_WIDTH - 1), 0), (0, 0)))
    ys = _mix_sample_call(sinks[l], xs, gate[b:b + db], zo_s, kv_s, kv_hist, u_hist, weights, nb=8)
    kvs3 = kv_s.reshape(db, ds, 2 * KV_DIM).astype(F32)
    k_new = _k_cols_inv(kvs3[..., :KV_DIM]).reshape(db, ds, N_KV, HEAD_DIM)
    v_new = kvs3[..., KV_DIM:].reshape(db, ds, N_KV, HEAD_DIM)
    new_k_s = jnp.concatenate([cache_k[l], k_new], axis=1)[:, -n_win:]
    new_v_s = jnp.concatenate([cache_v[l], v_new], axis=1)[:, -n_win:]
    u_new = zo_s.reshape(N_ZO_BLK, db, ds, CONV_DIM)[2].astype(F32)
    new_c_s = jnp.concatenate([state_conv[l], u_new], axis=1)[:, -(CONV_WIDTH - 1):]
    return yp, ys, new_k_p, new_v_p, new_c_p, new_k_s, new_v_s, new_c_s


def kernel(x_prompt, x_sample, c_prompt, c_sample, cache_k, cache_v, state_conv, norm_g, w_ada, b_ada,
           w_in, sinks, w_dw, b_dw, ln_g, ln_b, w_proj_a, w_proj_b, w_out, final_g):
    assert w_in.shape[0] == 1
    res = _layer(0, x_prompt, x_sample, c_prompt, c_sample, cache_k, cache_v, state_conv, norm_g, w_ada,
                 b_ada, w_in, sinks, w_dw, b_dw, ln_g, ln_b, w_proj_a, w_proj_b, w_out, final_g)
    return res[:2] + tuple(r[None] for r in res[2:])
```

```python
import jax
import jax.numpy as jnp
from jax import lax
from jax.experimental import pallas as pl
from jax.experimental.pallas import tpu as pltpu

D_MODEL = 2048
N_HEADS = 16
N_KV = 4
HEAD_DIM = 64
HALF = HEAD_DIM // 2
GROUP = N_HEADS // N_KV
ATTN_DIM = N_HEADS * HEAD_DIM
KV_DIM = N_KV * HEAD_DIM
CONV_DIM = 1024
CONV_WIDTH = 31
CHUNK = 64
WINDOW = 128
ROPE_THETA = 10000.0
RMS_EPS = 1e-6
LN_EPS = 1e-5
NEG_INF = -1e30
LOG2E = 1.4426950408889634
PAST_LEN = 1024

Q_END = ATTN_DIM
K_END = Q_END + KV_DIM
V_END = K_END + KV_DIM
GA_END = V_END + ATTN_DIM
CU_END = GA_END + 2 * CONV_DIM
GB_END = CU_END + CONV_DIM
IN_DIM = GB_END + 2 * D_MODEL

LANES = 128
HIST_U = 32
N_CT = CONV_DIM // LANES
VMEM_LIMIT = 56 * 1024 * 1024

N_ZO_BLK = 8
WBLK = 1024
N_HEAD_BLK = 2
N_WBLK = 9
ROW_SUB = 256
NORM_ROWS = 128

BF16 = jnp.bfloat16
F32 = jnp.float32


def _sigmoid(x):
    return 1.0 / (1.0 + jnp.exp(-x))


def _mod_kernel(c_ref, w_ref, b_ref, o_ref):
    c = c_ref[...]
    a = (c * _sigmoid(c)).astype(BF16)
    w = w_ref[...].astype(BF16)
    o_ref[...] = jnp.dot(a, w, preferred_element_type=F32) + b_ref[...]


def _mod_call(c_all, w_ada, b_ada):
    rows = c_all.shape[0]
    n = w_ada.shape[1]
    tn = 1024
    return pl.pallas_call(
        _mod_kernel,
        grid=(n // tn,),
        in_specs=[
            pl.BlockSpec((rows, D_MODEL), lambda j: (0, 0)),
            pl.BlockSpec((D_MODEL, tn), lambda j: (0, j)),
            pl.BlockSpec((1, tn), lambda j: (0, j)),
        ],
        out_specs=pl.BlockSpec((rows, tn), lambda j: (0, j)),
        out_shape=jax.ShapeDtypeStruct((rows, n), F32),
        compiler_params=pltpu.CompilerParams(
            dimension_semantics=("arbitrary",), vmem_limit_bytes=VMEM_LIMIT),
        name="adaln_mod",
    )(c_all, w_ada, b_ada)


def _rope_pair(a, b, cos, sin):
    return a * cos - b * sin, b * cos + a * sin


def _inproj_kernel(x_ref, shift_ref, scale_ref, g_ref, cos_ref, sin_ref, wh_ref, wl_ref, wu_ref, wkv_ref,
                   zo_ref, kv_ref, h_scr, glu_scr):
    j = pl.program_id(1)
    nb, t_rows, _ = x_ref.shape
    tm = nb * t_rows
    sub = min(tm, ROW_SUB)
    row_blocks = range(0, tm, sub)
    wr_ref = (wl_ref, wu_ref)

    def proj(w_ref, r0):
        h = h_scr[r0:r0 + sub, :]
        if isinstance(w_ref, tuple):
            return jnp.concatenate(
                [jnp.dot(h, w[...], preferred_element_type=F32) for w in w_ref], axis=1)
        return jnp.dot(h, w_ref[...], preferred_element_type=F32)

    def norm_rows(r0, rc):
        if rc <= t_rows:
            bi, t0 = divmod(r0, t_rows)
            x = x_ref[bi, t0:t0 + rc, :]
            scale, shift = scale_ref[bi], shift_ref[bi]
        else:
            b0, nbc = r0 // t_rows, rc // t_rows
            x = x_ref[b0:b0 + nbc]
            scale, shift = scale_ref[b0:b0 + nbc], shift_ref[b0:b0 + nbc]
        ms = jnp.mean(x * x, axis=-1, keepdims=True)
        y = x * lax.rsqrt(ms + RMS_EPS) * g_ref[...]
        h = y * (1.0 + scale) + shift
        h_scr[r0:r0 + rc, :] = h.reshape(rc, D_MODEL).astype(BF16)

    @pl.when(j == 0)
    def _():
        rc = min(sub, NORM_ROWS)
        for r0 in row_blocks:
            for c0 in range(r0, r0 + sub, rc):
                norm_rows(c0, rc)
            rows = slice(r0, r0 + sub)
            cos = cos_ref[rows, :]
            sin = sin_ref[rows, :]
            acc = jnp.dot(h_scr[rows, :], wkv_ref[...], preferred_element_type=F32)
            ka, kb = _rope_pair(acc[:, 0:LANES], acc[:, LANES:2 * LANES], cos, sin)
            kv_ref[rows, 0:LANES] = ka.astype(BF16)
            kv_ref[rows, LANES:2 * LANES] = kb.astype(BF16)
            kv_ref[rows, 2 * LANES:] = acc[:, 2 * LANES:].astype(BF16)
            acc = proj(wh_ref, r0)
            for i in range(GROUP):
                lo = i * 2 * LANES
                qa, qb = _rope_pair(acc[:, lo:lo + LANES], acc[:, lo + LANES:lo + 2 * LANES], cos, sin)
                zo_ref[rows, lo:lo + LANES] = qa.astype(BF16)
                zo_ref[rows, lo + LANES:lo + 2 * LANES] = qb.astype(BF16)

    @pl.when(j == 1)
    def _():
        for r0 in row_blocks:
            acc = proj(wh_ref, r0)
            zo_ref[r0:r0 + sub, :] = (acc * _sigmoid(acc)).astype(BF16)

    @pl.when(j == 2)
    def _():
        for r0 in row_blocks:
            glu_scr[r0:r0 + sub, :] = proj(wr_ref, r0)

    @pl.when(j == 3)
    def _():
        for r0 in row_blocks:
            zo_ref[r0:r0 + sub, :] = (glu_scr[r0:r0 + sub, :] * _sigmoid(proj(wr_ref, r0))).astype(BF16)

    @pl.when(j == 4)
    def _():
        for r0 in row_blocks:
            acc = proj(wr_ref, r0)
            zo_ref[r0:r0 + sub, :] = (acc * _sigmoid(acc)).astype(BF16)

    @pl.when(j >= 5)
    def _():
        for r0 in row_blocks:
            zo_ref[r0:r0 + sub, :] = _sigmoid(proj(wr_ref, r0)).astype(BF16)


def _inproj_call(x, shift, scale, norm_g, cos_t, sin_t, w_head, w_all, w_kv, *, nb, t_rows):
    half = WBLK // 2
    first_half = GA_END // half
    b, s, _ = x.shape
    tm = nb * t_rows
    tiles_per_batch = s // t_rows
    n_tiles = (b // nb) * tiles_per_batch
    pos_tiles = cos_t.shape[0] // tm

    def x_map(i, j):
        return (i // tiles_per_batch, i % tiles_per_batch, 0)

    def mod_map(i, j):
        return (i // tiles_per_batch, 0, 0)

    def pos_map(i, j):
        return (i % pos_tiles, 0)

    def zo_map(i, j):
        return (jnp.where(j < 3, j, j - 1), i, 0)

    return pl.pallas_call(
        _inproj_kernel,
        grid=(n_tiles, N_WBLK),
        in_specs=[
            pl.BlockSpec((nb, t_rows, D_MODEL), x_map),
            pl.BlockSpec((nb, 1, D_MODEL), mod_map),
            pl.BlockSpec((nb, 1, D_MODEL), mod_map),
            pl.BlockSpec((1, D_MODEL), lambda i, j: (0, 0)),
            pl.BlockSpec((tm, LANES), pos_map),
            pl.BlockSpec((tm, LANES), pos_map),
            pl.BlockSpec((D_MODEL, WBLK), lambda i, j: (0, jnp.minimum(j, N_HEAD_BLK - 1))),
            pl.BlockSpec((D_MODEL, half),
                         lambda i, j: (0, first_half + 2 * jnp.maximum(j - N_HEAD_BLK, 0))),
            pl.BlockSpec((D_MODEL, half),
                         lambda i, j: (0, first_half + 2 * jnp.maximum(j - N_HEAD_BLK, 0) + 1)),
            pl.BlockSpec((D_MODEL, 2 * KV_DIM), lambda i, j: (0, 0)),
        ],
        out_specs=[
            pl.BlockSpec((None, tm, WBLK), zo_map),
            pl.BlockSpec((tm, 2 * KV_DIM), lambda i, j: (i, 0)),
        ],
        out_shape=[
            jax.ShapeDtypeStruct((N_ZO_BLK, b * s, WBLK), BF16),
            jax.ShapeDtypeStruct((b * s, 2 * KV_DIM), BF16),
        ],
        scratch_shapes=[
            pltpu.VMEM((tm, D_MODEL), BF16),
            pltpu.VMEM((tm, WBLK), F32),
        ],
        compiler_params=pltpu.CompilerParams(
            dimension_semantics=("arbitrary", "arbitrary"), vmem_limit_bytes=VMEM_LIMIT),
        name="in_proj",
    )(x, shift, scale, norm_g, cos_t, sin_t, w_head, w_all, w_all, w_kv)


def _kv_group_masks():
    lane = lax.broadcasted_iota(jnp.int32, (1, 2 * LANES), 1)
    kmask = [((lane % LANES) // HALF) == g for g in range(N_KV)]
    vmask = [(lane // HEAD_DIM) == g for g in range(N_KV)]
    return kmask, vmask


def _sink_columns(sinks_ref, rows):
    cq = rows // GROUP
    rid = lax.broadcasted_iota(jnp.int32, (rows, 1), 0)
    cols = []
    for g in range(N_KV):
        sink = jnp.full((rows, 1), sinks_ref[GROUP * g + GROUP - 1], F32)
        for i in range(GROUP - 2, -1, -1):
            sink = jnp.where(rid < (i + 1) * cq, sinks_ref[GROUP * g + i], sink)
        cols.append(sink * LOG2E)
    return cols


def _attend_chunk(qs, kwin, vwin, sink_cols, invalid):
    out = None
    for g in range(N_KV):
        s = lax.dot_general(qs, kwin[g], (((1,), (1,)), ((), ())), preferred_element_type=F32)
        s = s * (HEAD_DIM ** -0.5 * LOG2E)
        if invalid is not None:
            s = jnp.where(invalid, NEG_INF, s)
        sink = sink_cols[g]
        m = jnp.maximum(jnp.max(s, axis=-1, keepdims=True), sink)
        e = jnp.exp2(s - m)
        denom = jnp.sum(e, axis=-1, keepdims=True) + jnp.exp2(sink - m)
        p = (e / denom).astype(BF16)
        o = jnp.dot(p, vwin[g], preferred_element_type=F32)
        out = o if out is None else out + o
    return out


def _exact_zero_after(v):
    bits = pltpu.bitcast(v[0:8, 0:LANES], jnp.uint32)
    return pltpu.bitcast((bits >> 16) >> 16, F32)


def _conv_ln_swish(ubuf, wdw_ref, bdw_ref, lng_ref, lnb_ref, y_scr, row0, n_rows, u_row0=0, after=None):
    first = HIST_U - (CONV_WIDTH - 1) + u_row0
    zeros = None if after is None else jnp.concatenate([after] * (n_rows // 8), axis=0)
    for ct in range(N_CT):
        cs = slice(ct * LANES, (ct + 1) * LANES)
        acc = jnp.broadcast_to(bdw_ref[:, cs], (n_rows, LANES))
        if zeros is not None:
            acc = acc + zeros
        for k in range(CONV_WIDTH):
            acc = acc + wdw_ref[k:k + 1, cs] * ubuf[ct, pl.ds(first + k, n_rows), :]
        y_scr[pl.ds(row0, n_rows), cs] = acc
    y = y_scr[pl.ds(row0, n_rows), :]
    mu = jnp.mean(y, axis=-1, keepdims=True)
    yc = y - mu
    var = jnp.mean(yc * yc, axis=-1, keepdims=True)
    yn = yc * lax.rsqrt(var + LN_EPS) * lng_ref[...] + lnb_ref[...]
    return yn * _sigmoid(yn)


def _out_tail(x, gate, ao, co, smg_ref, wa_ref, wb_ref, wo_ref, fg_ref):
    pa = jnp.dot(ao, wa_ref[...], preferred_element_type=F32)
    pb = jnp.dot(co, wb_ref[...], preferred_element_type=F32)
    sa = jnp.concatenate([smg_ref[0], smg_ref[1]], axis=1).astype(F32)
    sb = jnp.concatenate([smg_ref[2], smg_ref[3]], axis=1).astype(F32)
    merged = sa * pa + sb * pb
    o = jnp.dot(merged.astype(BF16), wo_ref[...], preferred_element_type=F32)
    r = x + gate * o
    ms = jnp.mean(r * r, axis=-1, keepdims=True)
    return r * lax.rsqrt(ms + RMS_EPS) * fg_ref[...]


def _mix_prompt_kernel(sinks_ref, x_ref, gate_ref, q_ref, sga_ref, u_ref, sgb_ref, smg_ref,
                       kv_ref, kvh_ref, uh_ref, wdw_ref, bdw_ref, lng_ref, lnb_ref,
                       wa_ref, wb_ref, wo_ref, fg_ref, y_ref,
                       km_scr, vm_scr, ubuf, cy_scr, ao_scr, co_scr):
    t = pl.program_id(1)
    tm = q_ref.shape[0]
    n_chunks = tm // CHUNK
    first_tile = t == 0

    kmask, vmask = _kv_group_masks()
    kh = kvh_ref[:, 0:KV_DIM]
    vh = kvh_ref[:, KV_DIM:]
    vh = jnp.where(first_tile, jnp.zeros_like(vh), vh)
    kc = kv_ref[:, 0:KV_DIM]
    vc = kv_ref[:, KV_DIM:]
    zero = jnp.zeros((), BF16)
    for g in range(N_KV):
        km_scr[g, 0:WINDOW, :] = jnp.where(kmask[g], kh, zero)
        km_scr[g, WINDOW:, :] = jnp.where(kmask[g], kc, zero)
        vm_scr[g, 0:WINDOW, :] = jnp.where(vmask[g], vh, zero)
        vm_scr[g, WINDOW:, :] = jnp.where(vmask[g], vc, zero)

    uh = uh_ref[...].astype(F32)
    uh = jnp.where(first_tile, jnp.zeros_like(uh), uh)
    for ct in range(N_CT):
        cs = slice(ct * LANES, (ct + 1) * LANES)
        ubuf[ct, 0:HIST_U, :] = uh[:, cs]
        ubuf[ct, HIST_U:, :] = u_ref[:, cs].astype(F32)

    nk = WINDOW + CHUNK
    key_id = lax.broadcasted_iota(jnp.int32, (1, nk), 1)
    sink_cols = _sink_columns(sinks_ref, GROUP * CHUNK)
    for c in range(n_chunks):
        r0 = c * CHUNK
        rows = slice(r0, r0 + CHUNK)
        qs = jnp.concatenate(
            [q_ref[rows, i * 2 * LANES:(i + 1) * 2 * LANES] for i in range(GROUP)], axis=0)
        kwin = [km_scr[g, r0:r0 + nk, :] for g in range(N_KV)]
        vwin = [vm_scr[g, r0:r0 + nk, :] for g in range(N_KV)]
        n_bad = WINDOW - r0
        invalid = (first_tile & (key_id < n_bad)) if n_bad > 0 else None
        o = _attend_chunk(qs, kwin, vwin, sink_cols, invalid)
        for i in range(GROUP):
            cs = slice(i * 2 * LANES, (i + 1) * 2 * LANES)
            ao_scr[rows, cs] = (o[i * CHUNK:(i + 1) * CHUNK, :] * sga_ref[rows, cs].astype(F32)).astype(BF16)
        act = _conv_ln_swish(ubuf, wdw_ref, bdw_ref, lng_ref, lnb_ref, cy_scr, r0, CHUNK,
                             u_row0=r0, after=_exact_zero_after(o))
        co_scr[rows, :] = (act * sgb_ref[rows, :].astype(F32)).astype(BF16)

    y_ref[0] = _out_tail(x_ref[0], gate_ref[0], ao_scr[...], co_scr[...], smg_ref,
                         wa_ref, wb_ref, wo_ref, fg_ref)


def _mix_sample_kernel(sinks_ref, x_ref, gate_ref, q_ref, sga_ref, u_ref, sgb_ref, smg_ref,
                       kv_ref, kvh_ref, uh_ref, wdw_ref, bdw_ref, lng_ref, lnb_ref,
                       wa_ref, wb_ref, wo_ref, fg_ref, y_ref,
                       ubuf, cy_scr, ao_scr, co_scr):
    nb, t_rows, _ = x_ref.shape
    kmask, vmask = _kv_group_masks()
    zero = jnp.zeros((), BF16)
    sink_cols = _sink_columns(sinks_ref, GROUP * t_rows)
    for b in range(nb):
        r0 = b * t_rows
        kall = jnp.concatenate([kvh_ref[b, :, 0:KV_DIM], kv_ref[r0:r0 + t_rows, 0:KV_DIM]], axis=0)
        vall = jnp.concatenate([kvh_ref[b, :, KV_DIM:], kv_ref[r0:r0 + t_rows, KV_DIM:]], axis=0)
        kwin = [jnp.where(kmask[g], kall, zero) for g in range(N_KV)]
        vwin = [jnp.where(vmask[g], vall, zero) for g in range(N_KV)]
        qs = jnp.concatenate(
            [q_ref[r0:r0 + t_rows, i * 2 * LANES:(i + 1) * 2 * LANES] for i in range(GROUP)], axis=0)
        o = _attend_chunk(qs, kwin, vwin, sink_cols, None)
        for i in range(GROUP):
            cs = slice(i * 2 * LANES, (i + 1) * 2 * LANES)
            ao_scr[r0:r0 + t_rows, cs] = (
                o[i * t_rows:(i + 1) * t_rows, :] * sga_ref[r0:r0 + t_rows, cs].astype(F32)).astype(BF16)

        for ct in range(N_CT):
            cs = slice(ct * LANES, (ct + 1) * LANES)
            ubuf[ct, 0:HIST_U, :] = uh_ref[b, :, cs]
            ubuf[ct, HIST_U:, :] = u_ref[r0:r0 + t_rows, cs].astype(F32)
        act = _conv_ln_swish(ubuf, wdw_ref, bdw_ref, lng_ref, lnb_ref, cy_scr, r0, t_rows)
        co_scr[r0:r0 + t_rows, :] = (act * sgb_ref[r0:r0 + t_rows, :].astype(F32)).astype(BF16)

    x = x_ref[...]
    out = _out_tail(x.reshape(nb * t_rows, D_MODEL),
                    jnp.broadcast_to(gate_ref[...], x.shape).reshape(nb * t_rows, D_MODEL),
                    ao_scr[...], co_scr[...], smg_ref, wa_ref, wb_ref, wo_ref, fg_ref)
    y_ref[...] = out.reshape(x.shape)


def _const_spec(shape):
    nd = len(shape)
    return pl.BlockSpec(shape, lambda *_: (0,) * nd, pipeline_mode=pl.Buffered(1))


def _weight_specs():
    return [
        _const_spec((HIST_U, CONV_DIM)),
        _const_spec((1, CONV_DIM)),
        _const_spec((1, CONV_DIM)),
        _const_spec((1, CONV_DIM)),
        _const_spec((ATTN_DIM, D_MODEL)),
        _const_spec((CONV_DIM, D_MODEL)),
        _const_spec((D_MODEL, D_MODEL)),
        _const_spec((1, D_MODEL)),
    ]


def _mix_prompt_call(sinks, x, gate, zo, kv, weights, *, tm):
    b, s, _ = x.shape
    tpb = s // tm

    def tile(bi, t):
        return bi * tpb + t

    def kvh_map(bi, t):
        return (jnp.maximum(tile(bi, t) * (tm // WINDOW) - 1, 0), 0)

    def uh_map(bi, t):
        return (2, jnp.maximum(tile(bi, t) * (tm // HIST_U) - 1, 0), 0)

    in_specs = [
        pl.BlockSpec(memory_space=pltpu.SMEM),
        pl.BlockSpec((1, tm, D_MODEL), lambda bi, t: (bi, t, 0)),
        pl.BlockSpec((1, 1, D_MODEL), lambda bi, t: (bi, 0, 0)),
        pl.BlockSpec((None, tm, WBLK), lambda bi, t: (0, tile(bi, t), 0)),
        pl.BlockSpec((None, tm, WBLK), lambda bi, t: (1, tile(bi, t), 0)),
        pl.BlockSpec((None, tm, WBLK), lambda bi, t: (2, tile(bi, t), 0)),
        pl.BlockSpec((None, tm, WBLK), lambda bi, t: (3, tile(bi, t), 0)),
        pl.BlockSpec((4, tm, WBLK), lambda bi, t: (1, tile(bi, t), 0)),
        pl.BlockSpec((tm, 2 * KV_DIM), lambda bi, t: (tile(bi, t), 0)),
        pl.BlockSpec((WINDOW, 2 * KV_DIM), kvh_map),
        pl.BlockSpec((None, HIST_U, WBLK), uh_map),
    ] + _weight_specs()
    return pl.pallas_call(
        _mix_prompt_kernel,
        grid=(b, tpb),
        in_specs=in_specs,
        out_specs=pl.BlockSpec((1, tm, D_MODEL), lambda bi, t: (bi, t, 0)),
        out_shape=jax.ShapeDtypeStruct(x.shape, F32),
        scratch_shapes=[
            pltpu.VMEM((N_KV, WINDOW + tm, KV_DIM), BF16),
            pltpu.VMEM((N_KV, WINDOW + tm, KV_DIM), BF16),
            pltpu.VMEM((N_CT, HIST_U + tm, LANES), F32),
            pltpu.VMEM((tm, CONV_DIM), F32),
            pltpu.VMEM((tm, ATTN_DIM), BF16),
            pltpu.VMEM((tm, CONV_DIM), BF16),
        ],
        compiler_params=pltpu.CompilerParams(
            dimension_semantics=("arbitrary", "arbitrary"), vmem_limit_bytes=VMEM_LIMIT),
        name="mix_prompt",
    )(sinks, x, gate, zo, zo, zo, zo, zo, kv, kv, zo, *weights)


def _mix_sample_call(sinks, x, gate, zo, kv, kv_hist, u_hist, weights, *, nb):
    b, t_rows, _ = x.shape
    tm = nb * t_rows
    in_specs = [
        pl.BlockSpec(memory_space=pltpu.SMEM),
        pl.BlockSpec((nb, t_rows, D_MODEL), lambda i: (i, 0, 0)),
        pl.BlockSpec((nb, 1, D_MODEL), lambda i: (i, 0, 0)),
        pl.BlockSpec((None, tm, WBLK), lambda i: (0, i, 0)),
        pl.BlockSpec((None, tm, WBLK), lambda i: (1, i, 0)),
        pl.BlockSpec((None, tm, WBLK), lambda i: (2, i, 0)),
        pl.BlockSpec((None, tm, WBLK), lambda i: (3, i, 0)),
        pl.BlockSpec((4, tm, WBLK), lambda i: (1, i, 0)),
        pl.BlockSpec((tm, 2 * KV_DIM), lambda i: (i, 0)),
        pl.BlockSpec((nb, WINDOW, 2 * KV_DIM), lambda i: (i, 0, 0)),
        pl.BlockSpec((nb, HIST_U, CONV_DIM), lambda i: (i, 0, 0)),
    ] + _weight_specs()
    return pl.pallas_call(
        _mix_sample_kernel,
        grid=(b // nb,),
        in_specs=in_specs,
        out_specs=pl.BlockSpec((nb, t_rows, D_MODEL), lambda i: (i, 0, 0)),
        out_shape=jax.ShapeDtypeStruct(x.shape, F32),
        scratch_shapes=[
            pltpu.VMEM((N_CT, HIST_U + t_rows, LANES), F32),
            pltpu.VMEM((tm, CONV_DIM), F32),
            pltpu.VMEM((tm, ATTN_DIM), BF16),
            pltpu.VMEM((tm, CONV_DIM), BF16),
        ],
        compiler_params=pltpu.CompilerParams(
            dimension_semantics=("arbitrary",), vmem_limit_bytes=VMEM_LIMIT),
        name="mix_sample",
    )(sinks, x, gate, zo, zo, zo, zo, zo, kv, kv_hist, u_hist, *weights)


def _q_cols(w):
    lead = w.shape[:-1]
    w = w.reshape(*lead, N_KV, GROUP, 2, HALF)
    return jnp.moveaxis(w, (-4, -3, -2), (-2, -4, -3)).reshape(*lead, ATTN_DIM)


def _k_cols(w):
    lead = w.shape[:-1]
    w = w.reshape(*lead, N_KV, 2, HALF)
    return jnp.swapaxes(w, -3, -2).reshape(*lead, KV_DIM)


def _k_cols_inv(w):
    lead = w.shape[:-1]
    w = w.reshape(*lead, 2, N_KV, HALF)
    return jnp.swapaxes(w, -3, -2).reshape(*lead, KV_DIM)


def _a_cols(w):
    lead = w.shape[:-1]
    w = w.reshape(*lead, N_KV, GROUP, HEAD_DIM)
    return jnp.swapaxes(w, -3, -2).reshape(*lead, ATTN_DIM)


def _rope_tables(pos):
    inv = ROPE_THETA ** (-2.0 * jnp.arange(HALF, dtype=F32) / HEAD_DIM)
    ang = pos.astype(F32)[:, None] * inv[None, :]
    return jnp.tile(jnp.cos(ang), (1, N_KV)), jnp.tile(jnp.sin(ang), (1, N_KV))


def _layer(l, xp, xs, c_prompt, c_sample, cache_k, cache_v, state_conv, norm_g, w_ada, b_ada, w_in,
           sinks, w_dw, b_dw, ln_g, ln_b, w_proj_a, w_proj_b, w_out, final_g):
    b, s, _ = xp.shape
    db, ds, _ = xs.shape
    n_win = cache_k.shape[2]
    in_tm = min(s, 1024)
    assert n_win == WINDOW and s % in_tm == 0 and s % 256 == 0 and db % 8 == 0 and ds % 16 == 0

    wi = w_in[l].astype(BF16)
    w_head = jnp.concatenate([_q_cols(wi[:, :Q_END]), _a_cols(wi[:, V_END:GA_END])], axis=1)
    w_kv = jnp.concatenate([_k_cols(wi[:, Q_END:K_END]), wi[:, K_END:V_END]], axis=1)
    wa = jnp.swapaxes(w_proj_a[l].reshape(N_KV, GROUP, HEAD_DIM, D_MODEL), 0, 1)
    wa = wa.reshape(ATTN_DIM, D_MODEL).astype(BF16)
    wb = w_proj_b[l].astype(BF16)
    wo = w_out[l].astype(BF16)
    wdw = jnp.pad(w_dw[l], ((0, HIST_U - CONV_WIDTH), (0, 0)))
    weights = (wdw, b_dw[l][None], ln_g[l][None], ln_b[l][None], wa, wb, wo, final_g[None, :])

    pad_rows = (-(b + db)) % 16
    c_all = jnp.concatenate([c_prompt, c_sample, jnp.zeros((pad_rows, D_MODEL), F32)], axis=0)
    mod = _mod_call(c_all, w_ada[l], b_ada[l][None])
    shift, scale, gate = (mod[:, i * D_MODEL:(i + 1) * D_MODEL][:, None, :] for i in range(3))

    cos_p, sin_p = _rope_tables(jnp.arange(s))
    cos_s, sin_s = _rope_tables(PAST_LEN + jnp.arange(ds))
    cos_s, sin_s = jnp.tile(cos_s, (db, 1)), jnp.tile(sin_s, (db, 1))

    zo_p, kv_p = _inproj_call(xp, shift[:b], scale[:b], norm_g[l][None], cos_p, sin_p,
                              w_head, wi, w_kv, nb=1, t_rows=in_tm)
    yp = _mix_prompt_call(sinks[l], xp, gate[:b], zo_p, kv_p, weights, tm=256)
    kv3 = kv_p.reshape(b, s, 2 * KV_DIM)[:, s - n_win:].astype(F32)
    new_k_p = _k_cols_inv(kv3[..., :KV_DIM]).reshape(b, n_win, N_KV, HEAD_DIM)
    new_v_p = kv3[..., KV_DIM:].reshape(b, n_win, N_KV, HEAD_DIM)
    new_c_p = zo_p.reshape(N_ZO_BLK, b, s, CONV_DIM)[2, :, s - (CONV_WIDTH - 1):].astype(F32)

    zo_s, kv_s = _inproj_call(xs, shift[b:b + db], scale[b:b + db], norm_g[l][None], cos_s, sin_s,
                              w_head, wi, w_kv, nb=db, t_rows=ds)
    kv_hist = jnp.concatenate(
        [_k_cols(cache_k[l].reshape(db, n_win, KV_DIM)), cache_v[l].reshape(db, n_win, KV_DIM)],
        axis=-1).astype(BF16)
    u_hist = jnp.pad(state_conv[l], ((0, 0), (HIST_U - (CONV_WIDTH - 1), 0), (0, 0)))
    ys = _mix_sample_call(sinks[l], xs, gate[b:b + db], zo_s, kv_s, kv_hist, u_hist, weights, nb=8)
    kvs3 = kv_s.reshape(db, ds, 2 * KV_DIM).astype(F32)
    k_new = _k_cols_inv(kvs3[..., :KV_DIM]).reshape(db, ds, N_KV, HEAD_DIM)
    v_new = kvs3[..., KV_DIM:].reshape(db, ds, N_KV, HEAD_DIM)
    new_k_s = jnp.concatenate([cache_k[l], k_new], axis=1)[:, -n_win:]
    new_v_s = jnp.concatenate([cache_v[l], v_new], axis=1)[:, -n_win:]
    u_new = zo_s.reshape(N_ZO_BLK, db, ds, CONV_DIM)[2].astype(F32)
    new_c_s = jnp.concatenate([state_conv[l], u_new], axis=1)[:, -(CONV_WIDTH - 1):]
    return yp, ys, new_k_p, new_v_p, new_c_p, new_k_s, new_v_s, new_c_s


def kernel(x_prompt, x_sample, c_prompt, c_sample, cache_k, cache_v, state_conv, norm_g, w_ada, b_ada,
           w_in, sinks, w_dw, b_dw, ln_g, ln_b, w_proj_a, w_proj_b, w_out, final_g):
    assert w_in.shape[0] == 1
    res = _layer(0, x_prompt, x_sample, c_prompt, c_sample, cache_k, cache_v, state_conv, norm_g, w_ada,
                 b_ada, w_in, sinks, w_dw, b_dw, ln_g, ln_b, w_proj_a, w_proj_b, w_out, final_g)
    return res[:2] + tuple(r[None] for r in res[2:])
```

```python
import jax
import jax.numpy as jnp
from jax import lax
from jax.experimental import pallas as pl
from jax.experimental.pallas import tpu as pltpu

D_MODEL = 2048
N_HEADS = 16
N_KV = 4
HEAD_DIM = 64
HALF = HEAD_DIM // 2
GROUP = N_HEADS // N_KV
ATTN_DIM = N_HEADS * HEAD_DIM
KV_DIM = N_KV * HEAD_DIM
CONV_DIM = 1024
CONV_WIDTH = 31
CHUNK = 64
WINDOW = 128
ROPE_THETA = 10000.0
RMS_EPS = 1e-6
LN_EPS = 1e-5
NEG_INF = -1e30
LOG2E = 1.4426950408889634
PAST_LEN = 1024

Q_END = ATTN_DIM
K_END = Q_END + KV_DIM
V_END = K_END + KV_DIM
GA_END = V_END + ATTN_DIM
CU_END = GA_END + 2 * CONV_DIM
GB_END = CU_END + CONV_DIM
IN_DIM = GB_END + 2 * D_MODEL

LANES = 128
HIST_U = 32
N_CT = CONV_DIM // LANES
VMEM_LIMIT = 56 * 1024 * 1024

N_ZO_BLK = 8
WBLK = 1024
N_HEAD_BLK = 2
N_WBLK = 9
ROW_SUB = 256
NORM_ROWS = 128

BF16 = jnp.bfloat16
F32 = jnp.float32


def _sigmoid(x):
    return 1.0 / (1.0 + jnp.exp(-x))


def _mod_kernel(c_ref, w_ref, b_ref, o_ref):
    c = c_ref[...]
    a = (c * _sigmoid(c)).astype(BF16)
    w = w_ref[...].astype(BF16)
    o_ref[...] = jnp.dot(a, w, preferred_element_type=F32) + b_ref[...]


def _mod_call(c_all, w_ada, b_ada):
    rows = c_all.shape[0]
    n = w_ada.shape[1]
    tn = 1024
    return pl.pallas_call(
        _mod_kernel,
        grid=(n // tn,),
        in_specs=[
            pl.BlockSpec((rows, D_MODEL), lambda j: (0, 0)),
            pl.BlockSpec((D_MODEL, tn), lambda j: (0, j)),
            pl.BlockSpec((1, tn), lambda j: (0, j)),
        ],
        out_specs=pl.BlockSpec((rows, tn), lambda j: (0, j)),
        out_shape=jax.ShapeDtypeStruct((rows, n), F32),
        compiler_params=pltpu.CompilerParams(
            dimension_semantics=("arbitrary",), vmem_limit_bytes=VMEM_LIMIT),
        name="adaln_mod",
    )(c_all, w_ada, b_ada)


def _rope_pair(a, b, cos, sin):
    return a * cos - b * sin, b * cos + a * sin


def _inproj_kernel(x_ref, shift_ref, scale_ref, g_ref, cos_ref, sin_ref, wh_ref, wl_ref, wu_ref, wkv_ref,
                   zo_ref, kv_ref, h_scr, glu_scr):
    j = pl.program_id(1)
    nb, t_rows, _ = x_ref.shape
    tm = nb * t_rows
    sub = min(tm, ROW_SUB)
    row_blocks = range(0, tm, sub)
    wr_ref = (wl_ref, wu_ref)

    def proj(w_ref, r0):
        h = h_scr[r0:r0 + sub, :]
        if isinstance(w_ref, tuple):
            return jnp.concatenate(
                [jnp.dot(h, w[...], preferred_element_type=F32) for w in w_ref], axis=1)
        return jnp.dot(h, w_ref[...], preferred_element_type=F32)

    def norm_rows(r0, rc):
        if rc <= t_rows:
            bi, t0 = divmod(r0, t_rows)
            x = x_ref[bi, t0:t0 + rc, :]
            scale, shift = scale_ref[bi], shift_ref[bi]
        else:
            b0, nbc = r0 // t_rows, rc // t_rows
            x = x_ref[b0:b0 + nbc]
            scale, shift = scale_ref[b0:b0 + nbc], shift_ref[b0:b0 + nbc]
        ms = jnp.mean(x * x, axis=-1, keepdims=True)
        y = x * lax.rsqrt(ms + RMS_EPS) * g_ref[...]
        h = y * (1.0 + scale) + shift
        h_scr[r0:r0 + rc, :] = h.reshape(rc, D_MODEL).astype(BF16)

    @pl.when(j == 0)
    def _():
        rc = min(sub, NORM_ROWS)
        for r0 in row_blocks:
            for c0 in range(r0, r0 + sub, rc):
                norm_rows(c0, rc)
            rows = slice(r0, r0 + sub)
            cos = cos_ref[rows, :]
            sin = sin_ref[rows, :]
            acc = jnp.dot(h_scr[rows, :], wkv_ref[...], preferred_element_type=F32)
            ka, kb = _rope_pair(acc[:, 0:LANES], acc[:, LANES:2 * LANES], cos, sin)
            kv_ref[rows, 0:LANES] = ka.astype(BF16)
            kv_ref[rows, LANES:2 * LANES] = kb.astype(BF16)
            kv_ref[rows, 2 * LANES:] = acc[:, 2 * LANES:].astype(BF16)
            acc = proj(wh_ref, r0)
            for i in range(GROUP):
                lo = i * 2 * LANES
                qa, qb = _rope_pair(acc[:, lo:lo + LANES], acc[:, lo + LANES:lo + 2 * LANES], cos, sin)
                zo_ref[rows, lo:lo + LANES] = qa.astype(BF16)
                zo_ref[rows, lo + LANES:lo + 2 * LANES] = qb.astype(BF16)

    @pl.when(j == 1)
    def _():
        for r0 in row_blocks:
            acc = proj(wh_ref, r0)
            zo_ref[r0:r0 + sub, :] = (acc * _sigmoid(acc)).astype(BF16)

    @pl.when(j == 2)
    def _():
        for r0 in row_blocks:
            glu_scr[r0:r0 + sub, :] = proj(wr_ref, r0)

    @pl.when(j == 3)
    def _():
        for r0 in row_blocks:
            zo_ref[r0:r0 + sub, :] = (glu_scr[r0:r0 + sub, :] * _sigmoid(proj(wr_ref, r0))).astype(BF16)

    @pl.when(j == 4)
    def _():
        for r0 in row_blocks:
            acc = proj(wr_ref, r0)
            zo_ref[r0:r0 + sub, :] = (acc * _sigmoid(acc)).astype(BF16)

    @pl.when(j >= 5)
    def _():
        for r0 in row_blocks:
            zo_ref[r0:r0 + sub, :] = _sigmoid(proj(wr_ref, r0)).astype(BF16)


def _inproj_call(x, shift, scale, norm_g, cos_t, sin_t, w_head, w_all, w_kv, *, nb, t_rows):
    half = WBLK // 2
    first_half = GA_END // half
    b, s, _ = x.shape
    tm = nb * t_rows
    tiles_per_batch = s // t_rows
    n_tiles = (b // nb) * tiles_per_batch
    pos_tiles = cos_t.shape[0] // tm

    def x_map(i, j):
        return (i // tiles_per_batch, i % tiles_per_batch, 0)

    def mod_map(i, j):
        return (i // tiles_per_batch, 0, 0)

    def pos_map(i, j):
        return (i % pos_tiles, 0)

    def zo_map(i, j):
        return (jnp.where(j < 3, j, j - 1), i, 0)

    return pl.pallas_call(
        _inproj_kernel,
        grid=(n_tiles, N_WBLK),
        in_specs=[
            pl.BlockSpec((nb, t_rows, D_MODEL), x_map),
            pl.BlockSpec((nb, 1, D_MODEL), mod_map),
            pl.BlockSpec((nb, 1, D_MODEL), mod_map),
            pl.BlockSpec((1, D_MODEL), lambda i, j: (0, 0)),
            pl.BlockSpec((tm, LANES), pos_map),
            pl.BlockSpec((tm, LANES), pos_map),
            pl.BlockSpec((D_MODEL, WBLK), lambda i, j: (0, jnp.minimum(j, N_HEAD_BLK - 1))),
            pl.BlockSpec((D_MODEL, half),
                         lambda i, j: (0, first_half + 2 * jnp.maximum(j - N_HEAD_BLK, 0))),
            pl.BlockSpec((D_MODEL, half),
                         lambda i, j: (0, first_half + 2 * jnp.maximum(j - N_HEAD_BLK, 0) + 1)),
            pl.BlockSpec((D_MODEL, 2 * KV_DIM), lambda i, j: (0, 0)),
        ],
        out_specs=[
            pl.BlockSpec((None, tm, WBLK), zo_map),
            pl.BlockSpec((tm, 2 * KV_DIM), lambda i, j: (i, 0)),
        ],
        out_shape=[
            jax.ShapeDtypeStruct((N_ZO_BLK, b * s, WBLK), BF16),
            jax.ShapeDtypeStruct((b * s, 2 * KV_DIM), BF16),
        ],
        scratch_shapes=[
            pltpu.VMEM((tm, D_MODEL), BF16),
            pltpu.VMEM((tm, WBLK), F32),
        ],
        compiler_params=pltpu.CompilerParams(
            dimension_semantics=("arbitrary", "arbitrary"), vmem_limit_bytes=VMEM_LIMIT),
        name="in_proj",
    )(x, shift, scale, norm_g, cos_t, sin_t, w_head, w_all, w_all, w_kv)


def _kv_group_masks():
    lane = lax.broadcasted_iota(jnp.int32, (1, 2 * LANES), 1)
    kmask = [((lane % LANES) // HALF) == g for g in range(N_KV)]
    vmask = [(lane // HEAD_DIM) == g for g in range(N_KV)]
    return kmask, vmask


def _sink_columns(sinks_ref, rows):
    cq = rows // GROUP
    rid = lax.broadcasted_iota(jnp.int32, (rows, 1), 0)
    cols = []
    for g in range(N_KV):
        sink = jnp.full((rows, 1), sinks_ref[GROUP * g + GROUP - 1], F32)
        for i in range(GROUP - 2, -1, -1):
            sink = jnp.where(rid < (i + 1) * cq, sinks_ref[GROUP * g + i], sink)
        cols.append(sink * LOG2E)
    return cols


def _attend_chunk(qs, kwin, vwin, sink_cols, invalid):
    out = None
    for g in range(N_KV):
        s = lax.dot_general(qs, kwin[g], (((1,), (1,)), ((), ())), preferred_element_type=F32)
        s = s * (HEAD_DIM ** -0.5 * LOG2E)
        if invalid is not None:
            s = jnp.where(invalid, NEG_INF, s)
        sink = sink_cols[g]
        m = jnp.maximum(jnp.max(s, axis=-1, keepdims=True), sink)
        e = jnp.exp2(s - m)
        denom = jnp.sum(e, axis=-1, keepdims=True) + jnp.exp2(sink - m)
        p = (e / denom).astype(BF16)
        o = jnp.dot(p, vwin[g], preferred_element_type=F32)
        out = o if out is None else out + o
    return out


def _exact_zero_after(v):
    bits = pltpu.bitcast(v[0:8, 0:LANES], jnp.uint32)
    return pltpu.bitcast((bits >> 16) >> 16, F32)


def _conv_ln_swish(ubuf, wdw_ref, bdw_ref, lng_ref, lnb_ref, y_scr, row0, n_rows, u_row0=0, after=None):
    first = HIST_U - (CONV_WIDTH - 1) + u_row0
    zeros = None if after is None else jnp.concatenate([after] * (n_rows // 8), axis=0)
    for ct in range(N_CT):
        cs = slice(ct * LANES, (ct + 1) * LANES)
        acc = jnp.broadcast_to(bdw_ref[:, cs], (n_rows, LANES))
        if zeros is not None:
            acc = acc + zeros
        for k in range(CONV_WIDTH):
            acc = acc + wdw_ref[k:k + 1, cs] * ubuf[ct, pl.ds(first + k, n_rows), :]
        y_scr[pl.ds(row0, n_rows), cs] = acc
    y = y_scr[pl.ds(row0, n_rows), :]
    mu = jnp.mean(y, axis=-1, keepdims=True)
    yc = y - mu
    var = jnp.mean(yc * yc, axis=-1, keepdims=True)
    yn = yc * lax.rsqrt(var + LN_EPS) * lng_ref[...] + lnb_ref[...]
    return yn * _sigmoid(yn)


def _out_tail(x, gate, ao, co, smg_ref, wa_ref, wb_ref, wo_ref, fg_ref):
    pa = jnp.dot(ao, wa_ref[...], preferred_element_type=F32)
    pb = jnp.dot(co, wb_ref[...], preferred_element_type=F32)
    sa = jnp.concatenate([smg_ref[0], smg_ref[1]], axis=1).astype(F32)
    sb = jnp.concatenate([smg_ref[2], smg_ref[3]], axis=1).astype(F32)
    merged = sa * pa + sb * pb
    o = jnp.dot(merged.astype(BF16), wo_ref[...], preferred_element_type=F32)
    r = x + gate * o
    ms = jnp.mean(r * r, axis=-1, keepdims=True)
    return r * lax.rsqrt(ms + RMS_EPS) * fg_ref[...]


def _mix_prompt_kernel(sinks_ref, x_ref, gate_ref, q_ref, sga_ref, u_ref, sgb_ref, smg_ref,
                       kv_ref, kvh_ref, uh_ref, wdw_ref, bdw_ref, lng_ref, lnb_ref,
                       wa_ref, wb_ref, wo_ref, fg_ref, y_ref,
                       km_scr, vm_scr, ubuf, cy_scr, ao_scr, co_scr):
    t = pl.program_id(1)
    tm = q_ref.shape[0]
    n_chunks = tm // CHUNK
    first_tile = t == 0

    kmask, vmask = _kv_group_masks()
    kh = kvh_ref[:, 0:KV_DIM]
    vh = kvh_ref[:, KV_DIM:]
    vh = jnp.where(first_tile, jnp.zeros_like(vh), vh)
    kc = kv_ref[:, 0:KV_DIM]
    vc = kv_ref[:, KV_DIM:]
    zero = jnp.zeros((), BF16)
    for g in range(N_KV):
        km_scr[g, 0:WINDOW, :] = jnp.where(kmask[g], kh, zero)
        km_scr[g, WINDOW:, :] = jnp.where(kmask[g], kc, zero)
        vm_scr[g, 0:WINDOW, :] = jnp.where(vmask[g], vh, zero)
        vm_scr[g, WINDOW:, :] = jnp.where(vmask[g], vc, zero)

    uh = uh_ref[...].astype(F32)
    uh = jnp.where(first_tile, jnp.zeros_like(uh), uh)
    for ct in range(N_CT):
        cs = slice(ct * LANES, (ct + 1) * LANES)
        ubuf[ct, 0:HIST_U, :] = uh[:, cs]
        ubuf[ct, HIST_U:, :] = u_ref[:, cs].astype(F32)

    nk = WINDOW + CHUNK
    key_id = lax.broadcasted_iota(jnp.int32, (1, nk), 1)
    sink_cols = _sink_columns(sinks_ref, GROUP * CHUNK)
    for c in range(n_chunks):
        r0 = c * CHUNK
        rows = slice(r0, r0 + CHUNK)
        qs = jnp.concatenate(
            [q_ref[rows, i * 2 * LANES:(i + 1) * 2 * LANES] for i in range(GROUP)], axis=0)
        kwin = [km_scr[g, r0:r0 + nk, :] for g in range(N_KV)]
        vwin = [vm_scr[g, r0:r0 + nk, :] for g in range(N_KV)]
        n_bad = WINDOW - r0
        invalid = (first_tile & (key_id < n_bad)) if n_bad > 0 else None
        o = _attend_chunk(qs, kwin, vwin, sink_cols, invalid)
        for i in range(GROUP):
            cs = slice(i * 2 * LANES, (i + 1) * 2 * LANES)
            ao_scr[rows, cs] = (o[i * CHUNK:(i + 1) * CHUNK, :] * sga_ref[rows, cs].astype(F32)).astype(BF16)
        act = _conv_ln_swish(ubuf, wdw_ref, bdw_ref, lng_ref, lnb_ref, cy_scr, r0, CHUNK,
                             u_row0=r0, after=_exact_zero_after(o))
        co_scr[rows, :] = (act * sgb_ref[rows, :].astype(F32)).astype(BF16)

    y_ref[0] = _out_tail(x_ref[0], gate_ref[0], ao_scr[...], co_scr[...], smg_ref,
                         wa_ref, wb_ref, wo_ref, fg_ref)


def _mix_sample_kernel(sinks_ref, x_ref, gate_ref, q_ref, sga_ref, u_ref, sgb_ref, smg_ref,
                       kv_ref, kvh_ref, uh_ref, wdw_ref, bdw_ref, lng_ref, lnb_ref,
                       wa_ref, wb_ref, wo_ref, fg_ref, y_ref,
                       ubuf, cy_scr, ao_scr, co_scr):
    nb, t_rows, _ = x_ref.shape
    kmask, vmask = _kv_group_masks()
    zero = jnp.zeros((), BF16)
    sink_cols = _sink_columns(sinks_ref, GROUP * t_rows)
    for b in range(nb):
        r0 = b * t_rows
        kall = jnp.concatenate([kvh_ref[b, :, 0:KV_DIM], kv_ref[r0:r0 + t_rows, 0:KV_DIM]], axis=0)
        vall = jnp.concatenate([kvh_ref[b, :, KV_DIM:], kv_ref[r0:r0 + t_rows, KV_DIM:]], axis=0)
        kwin = [jnp.where(kmask[g], kall, zero) for g in range(N_KV)]
        vwin = [jnp.where(vmask[g], vall, zero) for g in range(N_KV)]
        qs = jnp.concatenate(
            [q_ref[r0:r0 + t_rows, i * 2 * LANES:(i + 1) * 2 * LANES] for i in range(GROUP)], axis=0)
        o = _attend_chunk(qs, kwin, vwin, sink_cols, None)
        for i in range(GROUP):
            cs = slice(i * 2 * LANES, (i + 1) * 2 * LANES)
            ao_scr[r0:r0 + t_rows, cs] = (
                o[i * t_rows:(i + 1) * t_rows, :] * sga_ref[r0:r0 + t_rows, cs].astype(F32)).astype(BF16)

        for ct in range(N_CT):
            cs = slice(ct * LANES, (ct + 1) * LANES)
            ubuf[ct, 0:HIST_U, :] = uh_ref[b, :, cs]
            ubuf[ct, HIST_U:, :] = u_ref[r0:r0 + t_rows, cs].astype(F32)
        act = _conv_ln_swish(ubuf, wdw_ref, bdw_ref, lng_ref, lnb_ref, cy_scr, r0, t_rows,
                             after=_exact_zero_after(o))
        co_scr[r0:r0 + t_rows, :] = (act * sgb_ref[r0:r0 + t_rows, :].astype(F32)).astype(BF16)

    x = x_ref[...]
    out = _out_tail(x.reshape(nb * t_rows, D_MODEL),
                    jnp.broadcast_to(gate_ref[...], x.shape).reshape(nb * t_rows, D_MODEL),
                    ao_scr[...], co_scr[...], smg_ref, wa_ref, wb_ref, wo_ref, fg_ref)
    y_ref[...] = out.reshape(x.shape)


def _const_spec(shape):
    nd = len(shape)
    return pl.BlockSpec(shape, lambda *_: (0,) * nd, pipeline_mode=pl.Buffered(1))


def _weight_specs():
    return [
        _const_spec((HIST_U, CONV_DIM)),
        _const_spec((1, CONV_DIM)),
        _const_spec((1, CONV_DIM)),
        _const_spec((1, CONV_DIM)),
        _const_spec((ATTN_DIM, D_MODEL)),
        _const_spec((CONV_DIM, D_MODEL)),
        _const_spec((D_MODEL, D_MODEL)),
        _const_spec((1, D_MODEL)),
    ]


def _mix_prompt_call(sinks, x, gate, zo, kv, weights, *, tm):
    b, s, _ = x.shape
    tpb = s // tm

    def tile(bi, t):
        return bi * tpb + t

    def kvh_map(bi, t):
        return (jnp.maximum(tile(bi, t) * (tm // WINDOW) - 1, 0), 0)

    def uh_map(bi, t):
        return (2, jnp.maximum(tile(bi, t) * (tm // HIST_U) - 1, 0), 0)

    in_specs = [
        pl.BlockSpec(memory_space=pltpu.SMEM),
        pl.BlockSpec((1, tm, D_MODEL), lambda bi, t: (bi, t, 0)),
        pl.BlockSpec((1, 1, D_MODEL), lambda bi, t: (bi, 0, 0)),
        pl.BlockSpec((None, tm, WBLK), lambda bi, t: (0, tile(bi, t), 0)),
        pl.BlockSpec((None, tm, WBLK), lambda bi, t: (1, tile(bi, t), 0)),
        pl.BlockSpec((None, tm, WBLK), lambda bi, t: (2, tile(bi, t), 0)),
        pl.BlockSpec((None, tm, WBLK), lambda bi, t: (3, tile(bi, t), 0)),
        pl.BlockSpec((4, tm, WBLK), lambda bi, t: (1, tile(bi, t), 0)),
        pl.BlockSpec((tm, 2 * KV_DIM), lambda bi, t: (tile(bi, t), 0)),
        pl.BlockSpec((WINDOW, 2 * KV_DIM), kvh_map),
        pl.BlockSpec((None, HIST_U, WBLK), uh_map),
    ] + _weight_specs()
    return pl.pallas_call(
        _mix_prompt_kernel,
        grid=(b, tpb),
        in_specs=in_specs,
        out_specs=pl.BlockSpec((1, tm, D_MODEL), lambda bi, t: (bi, t, 0)),
        out_shape=jax.ShapeDtypeStruct(x.shape, F32),
        scratch_shapes=[
            pltpu.VMEM((N_KV, WINDOW + tm, KV_DIM), BF16),
            pltpu.VMEM((N_KV, WINDOW + tm, KV_DIM), BF16),
            pltpu.VMEM((N_CT, HIST_U + tm, LANES), F32),
            pltpu.VMEM((tm, CONV_DIM), F32),
            pltpu.VMEM((tm, ATTN_DIM), BF16),
            pltpu.VMEM((tm, CONV_DIM), BF16),
        ],
        compiler_params=pltpu.CompilerParams(
            dimension_semantics=("arbitrary", "arbitrary"), vmem_limit_bytes=VMEM_LIMIT),
        name="mix_prompt",
    )(sinks, x, gate, zo, zo, zo, zo, zo, kv, kv, zo, *weights)


def _mix_sample_call(sinks, x, gate, zo, kv, kv_hist, u_hist, weights, *, nb):
    b, t_rows, _ = x.shape
    tm = nb * t_rows
    in_specs = [
        pl.BlockSpec(memory_space=pltpu.SMEM),
        pl.BlockSpec((nb, t_rows, D_MODEL), lambda i: (i, 0, 0)),
        pl.BlockSpec((nb, 1, D_MODEL), lambda i: (i, 0, 0)),
        pl.BlockSpec((None, tm, WBLK), lambda i: (0, i, 0)),
        pl.BlockSpec((None, tm, WBLK), lambda i: (1, i, 0)),
        pl.BlockSpec((None, tm, WBLK), lambda i: (2, i, 0)),
        pl.BlockSpec((None, tm, WBLK), lambda i: (3, i, 0)),
        pl.BlockSpec((4, tm, WBLK), lambda i: (1, i, 0)),
        pl.BlockSpec((tm, 2 * KV_DIM), lambda i: (i, 0)),
        pl.BlockSpec((nb, WINDOW, 2 * KV_DIM), lambda i: (i, 0, 0)),
        pl.BlockSpec((nb, HIST_U, CONV_DIM), lambda i: (i, 0, 0)),
    ] + _weight_specs()
    return pl.pallas_call(
        _mix_sample_kernel,
        grid=(b // nb,),
        in_specs=in_specs,
        out_specs=pl.BlockSpec((nb, t_rows, D_MODEL), lambda i: (i, 0, 0)),
        out_shape=jax.ShapeDtypeStruct(x.shape, F32),
        scratch_shapes=[
            pltpu.VMEM((N_CT, HIST_U + t_rows, LANES), F32),
            pltpu.VMEM((tm, CONV_DIM), F32),
            pltpu.VMEM((tm, ATTN_DIM), BF16),
            pltpu.VMEM((tm, CONV_DIM), BF16),
        ],
        compiler_params=pltpu.CompilerParams(
            dimension_semantics=("arbitrary",), vmem_limit_bytes=VMEM_LIMIT),
        name="mix_sample",
    )(sinks, x, gate, zo, zo, zo, zo, zo, kv, kv_hist, u_hist, *weights)


def _q_cols(w):
    lead = w.shape[:-1]
    w = w.reshape(*lead, N_KV, GROUP, 2, HALF)
    return jnp.moveaxis(w, (-4, -3, -2), (-2, -4, -3)).reshape(*lead, ATTN_DIM)


def _k_cols(w):
    lead = w.shape[:-1]
    w = w.reshape(*lead, N_KV, 2, HALF)
    return jnp.swapaxes(w, -3, -2).reshape(*lead, KV_DIM)


def _k_cols_inv(w):
    lead = w.shape[:-1]
    w = w.reshape(*lead, 2, N_KV, HALF)
    return jnp.swapaxes(w, -3, -2).reshape(*lead, KV_DIM)


def _a_cols(w):
    lead = w.shape[:-1]
    w = w.reshape(*lead, N_KV, GROUP, HEAD_DIM)
    return jnp.swapaxes(w, -3, -2).reshape(*lead, ATTN_DIM)


def _rope_tables(pos):
    inv = ROPE_THETA ** (-2.0 * jnp.arange(HALF, dtype=F32) / HEAD_DIM)
    ang = pos.astype(F32)[:, None] * inv[None, :]
    return jnp.tile(jnp.cos(ang), (1, N_KV)), jnp.tile(jnp.sin(ang), (1, N_KV))


def _layer(l, xp, xs, c_prompt, c_sample, cache_k, cache_v, state_conv, norm_g, w_ada, b_ada, w_in,
           sinks, w_dw, b_dw, ln_g, ln_b, w_proj_a, w_proj_b, w_out, final_g):
    b, s, _ = xp.shape
    db, ds, _ = xs.shape
    n_win = cache_k.shape[2]
    in_tm = min(s, 1024)
    assert n_win == WINDOW and s % in_tm == 0 and s % 256 == 0 and db % 8 == 0 and ds % 16 == 0

    wi = w_in[l].astype(BF16)
    w_head = jnp.concatenate([_q_cols(wi[:, :Q_END]), _a_cols(wi[:, V_END:GA_END])], axis=1)
    w_kv = jnp.concatenate([_k_cols(wi[:, Q_END:K_END]), wi[:, K_END:V_END]], axis=1)
    wa = jnp.swapaxes(w_proj_a[l].reshape(N_KV, GROUP, HEAD_DIM, D_MODEL), 0, 1)
    wa = wa.reshape(ATTN_DIM, D_MODEL).astype(BF16)
    wb = w_proj_b[l].astype(BF16)
    wo = w_out[l].astype(BF16)
    wdw = jnp.pad(w_dw[l], ((0, HIST_U - CONV_WIDTH), (0, 0)))
    weights = (wdw, b_dw[l][None], ln_g[l][None], ln_b[l][None], wa, wb, wo, final_g[None, :])

    pad_rows = (-(b + db)) % 16
    c_all = jnp.concatenate([c_prompt, c_sample, jnp.zeros((pad_rows, D_MODEL), F32)], axis=0)
    mod = _mod_call(c_all, w_ada[l], b_ada[l][None])
    shift, scale, gate = (mod[:, i * D_MODEL:(i + 1) * D_MODEL][:, None, :] for i in range(3))

    cos_p, sin_p = _rope_tables(jnp.arange(s))
    cos_s, sin_s = _rope_tables(PAST_LEN + jnp.arange(ds))
    cos_s, sin_s = jnp.tile(cos_s, (db, 1)), jnp.tile(sin_s, (db, 1))

    zo_p, kv_p = _inproj_call(xp, shift[:b], scale[:b], norm_g[l][None], cos_p, sin_p,
                              w_head, wi, w_kv, nb=1, t_rows=in_tm)
    yp = _mix_prompt_call(sinks[l], xp, gate[:b], zo_p, kv_p, weights, tm=256)
    kv3 = kv_p.reshape(b, s, 2 * KV_DIM)[:, s - n_win:].astype(F32)
    new_k_p = _k_cols_inv(kv3[..., :KV_DIM]).reshape(b, n_win, N_KV, HEAD_DIM)
    new_v_p = kv3[..., KV_DIM:].reshape(b, n_win, N_KV, HEAD_DIM)
    new_c_p = zo_p.reshape(N_ZO_BLK, b, s, CONV_DIM)[2, :, s - (CONV_WIDTH - 1):].astype(F32)

    zo_s, kv_s = _inproj_call(xs, shift[b:b + db], scale[b:b + db], norm_g[l][None], cos_s, sin_s,
                              w_head, wi, w_kv, nb=db, t_rows=ds)
    kv_hist = jnp.concatenate(
        [_k_cols(cache_k[l].reshape(db, n_win, KV_DIM)), cache_v[l].reshape(db, n_win, KV_DIM)],
        axis=-1).astype(BF16)
    u_hist = jnp.pad(state_conv[l], ((0, 0), (HIST_U - (CONV_WIDTH - 1), 0), (0, 0)))
    ys = _mix_sample_call(sinks[l], xs, gate[b:b + db], zo_s, kv_s, kv_hist, u_hist, weights, nb=8)
    kvs3 = kv_s.reshape(db, ds, 2 * KV_DIM).astype(F32)
    k_new = _k_cols_inv(kvs3[..., :KV_DIM]).reshape(db, ds, N_KV, HEAD_DIM)
    v_new = kvs3[..., KV_DIM:].reshape(db, ds, N_KV, HEAD_DIM)
    new_k_s = jnp.concatenate([cache_k[l], k_new], axis=1)[:, -n_win:]
    new_v_s = jnp.concatenate([cache_v[l], v_new], axis=1)[:, -n_win:]
    u_new = zo_s.reshape(N_ZO_BLK, db, ds, CONV_DIM)[2].astype(F32)
    new_c_s = jnp.concatenate([state_conv[l], u_new], axis=1)[:, -(CONV_WIDTH - 1):]
    return yp, ys, new_k_p, new_v_p, new_c_p, new_k_s, new_v_s, new_c_s


def kernel(x_prompt, x_sample, c_prompt, c_sample, cache_k, cache_v, state_conv, norm_g, w_ada, b_ada,
           w_in, sinks, w_dw, b_dw, ln_g, ln_b, w_proj_a, w_proj_b, w_out, final_g):
    assert w_in.shape[0] == 1
    res = _layer(0, x_prompt, x_sample, c_prompt, c_sample, cache_k, cache_v, state_conv, norm_g, w_ada,
                 b_ada, w_in, sinks, w_dw, b_dw, ln_g, ln_b, w_proj_a, w_proj_b, w_out, final_g)
    return res[:2] + tuple(r[None] for r in res[2:])
```
